```python
import math
import jax
import jax.numpy as jnp
from jax import lax
import numpy as np

D_MODEL = 1024
BATCH = 8
SEQ = 4096
DEPTH = 4

GRID_W = 64
CTX_LEN = 256
N_MIXERS = 4
N_MOD = 6
EPS = 1e-6
FFN_HIDDEN = -(-8 * D_MODEL // (3 * 256)) * 256

GLA_HEADS = 4
GLA_QK = D_MODEL // 2
GLA_VD = D_MODEL
GLA_DK = GLA_QK // GLA_HEADS
GLA_DV = GLA_VD // GLA_HEADS
GLA_RANK = 16
GLA_GATE_NORM = 16.0
GLA_CHUNK = 64
GLA_IN = 2 * GLA_QK + GLA_VD + D_MODEL + 2 * GLA_RANK

HY_SHORT = 3
HY_BANDS = 16
HY_EMB = 1 + 2 * HY_BANDS
HY_FILTER_DIM = 64
HY_DECAY_TARGET = 1e-2
HY_FAST_DECAY = 0.3
HY_SLOW_DECAY = 1.5

ML_HEADS = 4
ML_INNER = 2 * D_MODEL
ML_DH = ML_INNER // ML_HEADS
ML_BLOCK = 4
ML_NBLOCKS = ML_INNER // ML_BLOCK
ML_CONV = 3
ML_CHUNK = 64

S5_GROUP = 16
S5_GROUPS = D_MODEL // S5_GROUP
S5_STATE = 64
S5_DT_MIN = 1e-3
S5_DT_MAX = 1e-1

N_GLA = (DEPTH + 3) // N_MIXERS
N_HY = (DEPTH + 2) // N_MIXERS
N_ML = (DEPTH + 1) // N_MIXERS
N_S5 = DEPTH // N_MIXERS

kernel_name = 'hybrid_gla_hyena_mlstm_s5_trunk'


def rms_norm(x, g):
    x = x.astype(jnp.float32)
    return x * lax.rsqrt(jnp.mean(x * x, axis=-1, keepdims=True) + EPS) * g.astype(jnp.float32)


def head_rms_norm(o, g):
    bsz, length, heads, d = o.shape
    o = o * lax.rsqrt(jnp.mean(o * o, axis=-1, keepdims=True) + EPS) * g
    return o.reshape(bsz, length, heads * d)


def multihead_layer_norm(o, g):
    bsz, length, heads, d = o.shape
    o = o - jnp.mean(o, axis=-1, keepdims=True)
    o = o * lax.rsqrt(jnp.mean(o * o, axis=-1, keepdims=True) + EPS)
    return o.reshape(bsz, length, heads * d) * g


def modulate(h, shift, scale):
    return h * (1.0 + scale) + shift


def flip_seq(t):
    return jnp.flip(t, axis=1)


def depthwise_conv_centred(x, w, b):
    k = w.shape[0]
    y = lax.conv_general_dilated(x, w.astype(x.dtype)[:, None, :], window_strides=(1,),
                                 padding=[(k // 2, k // 2)], dimension_numbers=('NWC', 'WIO', 'NWC'),
                                 feature_group_count=x.shape[-1])
    return y + b


def to_col_major(x, rows):
    bsz, length, ch = x.shape
    return x.reshape(bsz, rows, GRID_W, ch).transpose(0, 2, 1, 3).reshape(bsz, length, ch)


def from_col_major(x, rows):
    bsz, length, ch = x.shape
    return x.reshape(bsz, GRID_W, rows, ch).transpose(0, 2, 1, 3).reshape(bsz, length, ch)


def swiglu(h, w_in, w_out):
    gate, up = jnp.split(h @ w_in, 2, axis=-1)
    return (jax.nn.silu(gate) * up) @ w_out


def gla_chunk_scan(q, k, v, g, s0):
    bsz, length, heads, _ = q.shape
    dv = v.shape[-1]
    n_chunks = length // GLA_CHUNK

    def chunked(t):
        return t.reshape(bsz, n_chunks, GLA_CHUNK, heads, t.shape[-1]).swapaxes(0, 1)

    causal = jnp.tril(jnp.ones((GLA_CHUNK, GLA_CHUNK), dtype=bool))

    def step(s, inp):
        qc, kc, vc, gc = inp
        b = jnp.cumsum(gc, axis=1)
        b_end = b[:, -1]
        q_dec = qc * jnp.exp(b)
        scores = jnp.einsum('bihd,bjhd->bhij', q_dec, kc * jnp.exp(-b))
        scores = jnp.where(causal, scores, 0.0)
        o = jnp.einsum('bhij,bjhv->bihv', scores, vc) + jnp.einsum('bihd,bhdv->bihv', q_dec, s)
        k_end = kc * jnp.exp(b_end[:, None] - b)
        s = jnp.exp(b_end)[..., None] * s + jnp.einsum('bjhd,bjhv->bhdv', k_end, vc)
        return s, o

    s_end, o = lax.scan(step, s0, (chunked(q), chunked(k), chunked(v), chunked(g)))
    return o.swapaxes(0, 1).reshape(bsz, length, heads, dv), s_end


def gla_mixer(h_lat, h_ctx, w_in, w_gate, b_gate, norm_g, w_out):
    def project(h):
        bsz, length, _ = h.shape
        z = h.astype(jnp.float32) @ w_in
        q, k, v, r, glr = jnp.split(z, [GLA_QK, 2 * GLA_QK, 2 * GLA_QK + GLA_VD,
                                       2 * GLA_QK + GLA_VD + D_MODEL], axis=-1)
        glr = glr.reshape(bsz, length, 2, GLA_RANK)
        g = jax.nn.log_sigmoid(jnp.einsum('blzr,zrk->blzk', glr, w_gate) + b_gate) / GLA_GATE_NORM
        g = g.reshape(bsz, length, 2, GLA_HEADS, GLA_DK)
        q = q.reshape(bsz, length, GLA_HEADS, GLA_DK) * GLA_DK ** -0.5
        k = k.reshape(bsz, length, GLA_HEADS, GLA_DK)
        v = v.reshape(bsz, length, GLA_HEADS, GLA_DV)
        return q, k, v, r, g[:, :, 0], g[:, :, 1]

    def bidirectional(h, s_f0, s_b0):
        q, k, v, r, g_f, g_b = project(h)
        o_f, s_f = gla_chunk_scan(q, k, v, g_f, s_f0)
        o_b, s_b = gla_chunk_scan(flip_seq(q), flip_seq(k), flip_seq(v), flip_seq(g_b), s_b0)
        o = head_rms_norm(o_f + flip_seq(o_b), norm_g)
        return (o * jax.nn.silu(r)) @ w_out, s_f, s_b

    s0 = jnp.zeros((h_ctx.shape[0], GLA_HEADS, GLA_DK, GLA_DV), jnp.float32)
    y_ctx, s_f, s_b = bidirectional(h_ctx, s0, s0)
    y_lat, _, _ = bidirectional(h_lat, s_f, s_b)
    return y_lat, y_ctx


def hyena_filter(length, w1, b1, w2, b2, w3, freq):
    pos = jnp.arange(length, dtype=jnp.float32)
    t = pos / length
    bands = jnp.linspace(1e-4, HY_BANDS - 1, HY_BANDS, dtype=jnp.float32)
    ang = 2.0 * math.pi * t[:, None] * bands[None]
    feats = jnp.concatenate([t[:, None], jnp.cos(ang), -jnp.sin(ang)], axis=-1)
    z = jnp.sin(freq * (feats @ w1 + b1))
    z = jnp.sin(freq * (z @ w2 + b2))
    h = (z @ w3).reshape(length, 2, D_MODEL)
    log_target = math.log(HY_DECAY_TARGET)
    deltas = jnp.abs(jnp.linspace(log_target / HY_SLOW_DECAY, log_target / HY_FAST_DECAY, D_MODEL,
                                  dtype=jnp.float32))
    decay = jnp.exp(-t[:, None] * deltas)
    h = h * decay[:, None]
    return h[:, 0], h[:, 1]


def fft_long_conv(u, h_f, h_b, bias):
    length = u.shape[1]
    h2 = jnp.concatenate([h_f[:1] + h_b[:1], h_f[1:], jnp.zeros_like(h_f[:1]),
                          jnp.flip(h_b[1:], axis=0)], axis=0)
    u_hat = jnp.fft.rfft(u, n=2 * length, axis=1)
    h_hat = jnp.fft.rfft(h2, axis=0)
    y = jnp.fft.irfft(u_hat * h_hat[None], n=2 * length, axis=1)[:, :length]
    return y + u * bias


def hyena_mixer(h_lat, h_ctx, w_in, conv_w, conv_b, f_w1, f_b1, f_w2, f_b2, f_w3, f_freq, f_bias, w_out):
    def run(h):
        length = h.shape[1]
        u = depthwise_conv_centred(h.astype(jnp.float32) @ w_in, conv_w, conv_b)
        x0, x1, v = jnp.split(u, 3, axis=-1)
        h_f, h_b = hyena_filter(length, f_w1, f_b1, f_w2, f_b2, f_w3, f_freq)
        y = x0 * fft_long_conv(v * x1, h_f, h_b, f_bias)
        return y @ w_out

    return run(h_lat), run(h_ctx)


def mlstm_chunk_scan(q, k, v, i_pre, log_f, state):
    bsz, length, heads, _ = q.shape
    n_chunks = length // ML_CHUNK

    def chunked(t):
        return t.reshape((bsz, n_chunks, ML_CHUNK) + t.shape[2:]).swapaxes(0, 1)

    causal = jnp.tril(jnp.ones((ML_CHUNK, ML_CHUNK), dtype=bool))[None, :, :, None]

    def step(carry, inp):
        c_mat, n_vec, m = carry
        qc, kc, vc, ic, fc = inp
        b = jnp.cumsum(fc, axis=1)
        b_end = b[:, -1]
        d_intra = jnp.where(causal, b[:, :, None] - b[:, None] + ic[:, None], -jnp.inf)
        d_inter = b + m[:, None]
        m_tok = jnp.maximum(jnp.max(d_intra, axis=2), d_inter)
        w_inter = jnp.exp(d_inter - m_tok)
        s = jnp.einsum('bihd,bjhd->bijh', qc, kc) * jnp.exp(d_intra - m_tok[:, :, None])
        num = jnp.einsum('bijh,bjhv->bihv', s, vc) + w_inter[..., None] * jnp.einsum('bihd,bhdv->bihv', qc, c_mat)
        den = jnp.sum(s, axis=2) + w_inter * jnp.einsum('bihd,bhd->bih', qc, n_vec)
        h = num / jnp.maximum(jnp.abs(den), jnp.exp(-m_tok))[..., None]
        d_state = b_end[:, None] - b + ic
        m_new = jnp.maximum(b_end + m, jnp.max(d_state, axis=1))
        w_prev = jnp.exp(b_end + m - m_new)
        w_tok = jnp.exp(d_state - m_new[:, None])
        c_mat = w_prev[..., None, None] * c_mat + jnp.einsum('bjh,bjhd,bjhv->bhdv', w_tok, kc, vc)
        n_vec = w_prev[..., None] * n_vec + jnp.einsum('bjh,bjhd->bhd', w_tok, kc)
        return (c_mat, n_vec, m_new), h

    state, h = lax.scan(step, state, (chunked(q), chunked(k), chunked(v), chunked(i_pre), chunked(log_f)))
    return h.swapaxes(0, 1).reshape(bsz, length, heads, v.shape[-1]), state


def mlstm_mixer(h_lat, h_ctx, w_in, conv_w, conv_b, w_q, w_k, w_v, w_gates, b_gates, norm_g, skip, w_out):
    def project(h):
        bsz, length, _ = h.shape
        x_m, z = jnp.split(h.astype(jnp.float32) @ w_in, 2, axis=-1)
        x_c = jax.nn.silu(depthwise_conv_centred(x_m, conv_w, conv_b))

        def block_diag(t, w):
            t = t.reshape(bsz, length, ML_NBLOCKS, ML_BLOCK)
            return jnp.einsum('blnc,ncd->blnd', t, w).reshape(bsz, length, ML_INNER)

        q, k, v = block_diag(x_c, w_q), block_diag(x_c, w_k), block_diag(x_m, w_v)
        gates = jnp.einsum('blc,zcg->blzg', jnp.concatenate([q, k, v], axis=-1), w_gates) + b_gates
        i_pre = gates[..., :ML_HEADS]
        log_f = jax.nn.log_sigmoid(gates[..., ML_HEADS:])
        heads = lambda t: t.reshape(bsz, length, ML_HEADS, ML_DH)
        return heads(q), heads(k) * ML_DH ** -0.5, heads(v), i_pre, log_f, x_c, z

    def bidirectional(h, state_f, state_b):
        q, k, v, i_pre, log_f, x_c, z = project(h)
        h_f, state_f = mlstm_chunk_scan(q, k, v, i_pre[:, :, 0], log_f[:, :, 0], state_f)
        h_b, state_b = mlstm_chunk_scan(flip_seq(q), flip_seq(k), flip_seq(v), flip_seq(i_pre[:, :, 1]),
                                        flip_seq(log_f[:, :, 1]), state_b)
        h_n = multihead_layer_norm(h_f + flip_seq(h_b), norm_g)
        return ((h_n + skip * x_c) * jax.nn.silu(z)) @ w_out, state_f, state_b

    bsz = h_ctx.shape[0]
    state0 = (jnp.zeros((bsz, ML_HEADS, ML_DH, ML_DH), jnp.float32),
              jnp.zeros((bsz, ML_HEADS, ML_DH), jnp.float32),
              jnp.full((bsz, ML_HEADS), -jnp.inf, jnp.float32))
    y_ctx, st_f, st_b = bidirectional(h_ctx, state0, state0)
    y_lat, _, _ = bidirectional(h_lat, st_f, st_b)
    return y_lat, y_ctx


def _linear_recurrence_combine(left, right):
    a_l, b_l = left
    a_r, b_r = right
    return a_l * a_r, a_r * b_l + b_r


def s5_mixer(h_lat, h_ctx, lam_re, lam_im, log_dt, b_re, b_im, c_re, c_im, d_skip, w_glu):
    f32 = jnp.float32
    b_mat = lax.complex(b_re.astype(f32), b_im.astype(f32))
    c_mat = lax.complex(c_re.astype(f32), c_im.astype(f32))

    def discretise(direction):
        lam = lax.complex(lam_re[direction].astype(f32), lam_im[direction].astype(f32))
        lam_bar = jnp.exp(lam * jnp.exp(log_dt[direction].astype(f32))[:, None])
        return lam_bar, ((lam_bar - 1.0) / lam)[..., None] * b_mat

    disc = (discretise(0), discretise(1))

    def scan_readout(u_seq, lam_bar, b_bar, x0):
        bu = jnp.einsum('gpc,lbgc->lbgp', b_bar, u_seq)
        a = jnp.broadcast_to(lam_bar, (u_seq.shape[0], 1) + lam_bar.shape)
        a_cum, xs = lax.associative_scan(_linear_recurrence_combine, (a, bu), axis=0)
        xs = xs + a_cum * x0
        return jnp.real(jnp.einsum('gcp,lbgp->lbgc', c_mat, xs)), xs[-1]

    def bidirectional(h, x0_f, x0_b):
        bsz, length, _ = h.shape
        u = h.astype(f32)
        u_seq = u.swapaxes(0, 1).reshape(length, bsz, S5_GROUPS, S5_GROUP).astype(jnp.complex64)
        y_f, x_f = scan_readout(u_seq, disc[0][0], disc[0][1], x0_f)
        y_b, x_b = scan_readout(jnp.flip(u_seq, axis=0), disc[1][0], disc[1][1], x0_b)
        y = (y_f + jnp.flip(y_b, axis=0)).reshape(length, bsz, D_MODEL).swapaxes(0, 1) + d_skip * u
        val, gate = jnp.split(jax.nn.gelu(y) @ w_glu, 2, axis=-1)
        return val * jax.nn.sigmoid(gate), x_f, x_b

    x0 = jnp.zeros((h_ctx.shape[0], S5_GROUPS, S5_STATE), jnp.complex64)
    y_ctx, x_f, x_b = bidirectional(h_ctx, x0, x0)
    y_lat, _, _ = bidirectional(h_lat, x_f, x_b)
    return y_lat, y_ctx


def setup_inputs(seed: int = 0) -> dict:
    key = jax.random.key(seed)
    keys = iter(jax.random.split(key, 96))
    f32 = jnp.float32

    def normal(shape, scale=1.0):
        return scale * jax.random.normal(next(keys), shape, f32)

    def gain(shape):
        return 1.0 + normal(shape, 0.02)

    D, F = D_MODEL, FFN_HIDDEN
    ml_b_i = normal((N_ML, 2, ML_HEADS), 0.1)
    ml_b_f = jnp.linspace(3.0, 6.0, ML_HEADS, dtype=f32) + normal((N_ML, 2, ML_HEADS), 0.1)
    lam_im_init = math.pi * jnp.arange(S5_STATE, dtype=f32)
    return {
        'x': normal((BATCH, SEQ, D)),
        'c': normal((BATCH, D)),
        'ctx': normal((BATCH, CTX_LEN, D)),
        'c_ctx': normal((D,)),
        'mod_w': normal((DEPTH, D, N_MOD * D), D ** -0.5),
        'mod_b': normal((DEPTH, N_MOD * D), 0.01),
        'norm_g': gain((DEPTH, 4, D)),
        'ffn_w_in': normal((DEPTH, D, 2 * F), D ** -0.5),
        'ffn_w_out': normal((DEPTH, F, D), F ** -0.5),
        'gla_w_in': normal((N_GLA, D, GLA_IN), D ** -0.5),
        'gla_w_gate': normal((N_GLA, 2, GLA_RANK, GLA_QK), GLA_RANK ** -0.5),
        'gla_b_gate': normal((N_GLA, 2, GLA_QK), 0.1),
        'gla_norm_g': gain((N_GLA, GLA_DV)),
        'gla_w_out': normal((N_GLA, GLA_VD, D), GLA_VD ** -0.5),
        'hy_w_in': normal((N_HY, D, 3 * D), D ** -0.5),
        'hy_conv_w': normal((N_HY, HY_SHORT, 3 * D), HY_SHORT ** -0.5),
        'hy_conv_b': normal((N_HY, 3 * D), 0.01),
        'hy_f_w1': normal((N_HY, HY_EMB, HY_FILTER_DIM), HY_EMB ** -0.5),
        'hy_f_b1': normal((N_HY, HY_FILTER_DIM), 0.1),
        'hy_f_w2': normal((N_HY, HY_FILTER_DIM, HY_FILTER_DIM), HY_FILTER_DIM ** -0.5),
        'hy_f_b2': normal((N_HY, HY_FILTER_DIM), 0.1),
        'hy_f_w3': normal((N_HY, HY_FILTER_DIM, 2 * D), HY_FILTER_DIM ** -0.5),
        'hy_f_freq': gain((N_HY, HY_FILTER_DIM)),
        'hy_f_bias': normal((N_HY, D), 0.1),
        'hy_w_out': normal((N_HY, D, D), D ** -0.5),
        'ml_w_in': normal((N_ML, D, 2 * ML_INNER), D ** -0.5),
        'ml_conv_w': normal((N_ML, ML_CONV, ML_INNER), ML_CONV ** -0.5),
        'ml_conv_b': normal((N_ML, ML_INNER), 0.01),
        'ml_w_q': normal((N_ML, ML_NBLOCKS, ML_BLOCK, ML_BLOCK), ML_BLOCK ** -0.5),
        'ml_w_k': normal((N_ML, ML_NBLOCKS, ML_BLOCK, ML_BLOCK), ML_BLOCK ** -0.5),
        'ml_w_v': normal((N_ML, ML_NBLOCKS, ML_BLOCK, ML_BLOCK), ML_BLOCK ** -0.5),
        'ml_w_gates': normal((N_ML, 2, 3 * ML_INNER, 2 * ML_HEADS), (3 * ML_INNER) ** -0.5),
        'ml_b_gates': jnp.concatenate([ml_b_i, ml_b_f], axis=-1),
        'ml_norm_g': gain((N_ML, ML_INNER)),
        'ml_skip': gain((N_ML, ML_INNER)),
        'ml_w_out': normal((N_ML, ML_INNER, D), ML_INNER ** -0.5),
        's5_lam_re': -0.5 + normal((N_S5, 2, S5_GROUPS, S5_STATE), 0.01),
        's5_lam_im': lam_im_init + normal((N_S5, 2, S5_GROUPS, S5_STATE), 0.01),
        's5_log_dt': jax.random.uniform(next(keys), (N_S5, 2, S5_GROUPS), f32,
                                        minval=math.log(S5_DT_MIN), maxval=math.log(S5_DT_MAX)),
        's5_b_re': normal((N_S5, S5_GROUPS, S5_STATE, S5_GROUP), (2 * S5_GROUP) ** -0.5),
        's5_b_im': normal((N_S5, S5_GROUPS, S5_STATE, S5_GROUP), (2 * S5_GROUP) ** -0.5),
        's5_c_re': normal((N_S5, S5_GROUPS, S5_GROUP, S5_STATE), (2 * S5_STATE) ** -0.5),
        's5_c_im': normal((N_S5, S5_GROUPS, S5_GROUP, S5_STATE), (2 * S5_STATE) ** -0.5),
        's5_d': normal((N_S5, D)),
        's5_w_glu': normal((N_S5, D, 2 * D), D ** -0.5),
    }


def reference(x, c, ctx, c_ctx, mod_w, mod_b, norm_g, ffn_w_in, ffn_w_out,
              gla_w_in, gla_w_gate, gla_b_gate, gla_norm_g, gla_w_out,
              hy_w_in, hy_conv_w, hy_conv_b, hy_f_w1, hy_f_b1, hy_f_w2, hy_f_b2, hy_f_w3, hy_f_freq, hy_f_bias,
              hy_w_out,
              ml_w_in, ml_conv_w, ml_conv_b, ml_w_q, ml_w_k, ml_w_v, ml_w_gates, ml_b_gates, ml_norm_g, ml_skip,
              ml_w_out,
              s5_lam_re, s5_lam_im, s5_log_dt, s5_b_re, s5_b_im, s5_c_re, s5_c_im, s5_d, s5_w_glu):
    bsz = x.shape[0]
    rows = x.shape[1] // GRID_W
    lat, cx = x, ctx
    silu_c = jax.nn.silu(c.astype(jnp.float32))
    silu_cc = jax.nn.silu(c_ctx.astype(jnp.float32))
    for i in range(DEPTH):
        kind, j = i % N_MIXERS, i // N_MIXERS
        last = i == DEPTH - 1
        mod_l = (silu_c @ mod_w[i] + mod_b[i]).reshape(bsz, N_MOD, 1, D_MODEL)
        mod_c = (silu_cc @ mod_w[i] + mod_b[i]).reshape(N_MOD, 1, D_MODEL)
        h_lat = modulate(rms_norm(lat, norm_g[i, 0]), mod_l[:, 0], mod_l[:, 1])
        h_ctx = modulate(rms_norm(cx, norm_g[i, 0]), mod_c[0], mod_c[1])
        if kind == 0:
            y_lat, y_ctx = gla_mixer(h_lat, h_ctx, gla_w_in[j], gla_w_gate[j], gla_b_gate[j], gla_norm_g[j],
                                     gla_w_out[j])
        elif kind == 1:
            y_lat, y_ctx = hyena_mixer(h_lat, h_ctx, hy_w_in[j], hy_conv_w[j], hy_conv_b[j], hy_f_w1[j],
                                       hy_f_b1[j], hy_f_w2[j], hy_f_b2[j], hy_f_w3[j], hy_f_freq[j],
                                       hy_f_bias[j], hy_w_out[j])
        elif kind == 2:
            y_lat, y_ctx = mlstm_mixer(to_col_major(h_lat, rows), h_ctx, ml_w_in[j], ml_conv_w[j], ml_conv_b[j],
                                       ml_w_q[j], ml_w_k[j], ml_w_v[j], ml_w_gates[j], ml_b_gates[j],
                                       ml_norm_g[j], ml_skip[j], ml_w_out[j])
            y_lat = from_col_major(y_lat, rows)
        else:
            y_lat, y_ctx = s5_mixer(to_col_major(h_lat, rows), h_ctx, s5_lam_re[j], s5_lam_im[j], s5_log_dt[j],
                                    s5_b_re[j], s5_b_im[j], s5_c_re[j], s5_c_im[j], s5_d[j], s5_w_glu[j])
            y_lat = from_col_major(y_lat, rows)
        lat = lat + (mod_l[:, 2] * rms_norm(y_lat, norm_g[i, 1])).astype(lat.dtype)
        h = modulate(rms_norm(lat, norm_g[i, 2]), mod_l[:, 3], mod_l[:, 4])
        lat = lat + (mod_l[:, 5] * rms_norm(swiglu(h, ffn_w_in[i], ffn_w_out[i]), norm_g[i, 3])).astype(lat.dtype)
        if not last:
            cx = cx + (mod_c[2] * rms_norm(y_ctx, norm_g[i, 1])).astype(cx.dtype)
            hc = modulate(rms_norm(cx, norm_g[i, 2]), mod_c[3], mod_c[4])
            cx = cx + (mod_c[5] * rms_norm(swiglu(hc, ffn_w_in[i], ffn_w_out[i]), norm_g[i, 3])).astype(cx.dtype)
    return lat
```

```python
import functools
import math

import jax
import jax.numpy as jnp
from jax import lax
from jax.experimental import pallas as pl
from jax.experimental.pallas import tpu as pltpu

F32 = jnp.float32
BF16 = jnp.bfloat16

D_MODEL = 1024
GRID_W = 64
N_MOD = 6
EPS = 1e-6
FFN_HIDDEN = 2816
FFN_CHUNK = 256
N_FFN_CHUNKS = FFN_HIDDEN // FFN_CHUNK
MOD_ROWS = 16
VMEM_LIMIT = 56 * 1024 * 1024

N_MIXERS = 4
GLA_HEADS = 4
GLA_QK = D_MODEL // 2
GLA_VD = D_MODEL
GLA_DK = GLA_QK // GLA_HEADS
GLA_DV = GLA_VD // GLA_HEADS
GLA_RANK = 16
GLA_GATE_NORM = 16.0
GLA_CHUNK = 64
HY_BANDS = 16
HY_DECAY_TARGET = 1e-2
HY_FAST_DECAY = 0.3
HY_SLOW_DECAY = 1.5
ML_HEADS = 4
ML_INNER = 2 * D_MODEL
ML_DH = ML_INNER // ML_HEADS
ML_BLOCK = 4
ML_NBLOCKS = ML_INNER // ML_BLOCK
ML_CHUNK = 64
S5_GROUP = 16
S5_GROUPS = D_MODEL // S5_GROUP
S5_STATE = 64


def _rms(x, g):
    return x * lax.rsqrt(jnp.mean(x * x, axis=-1, keepdims=True) + EPS) * g


def _silu(x):
    return x * jax.nn.sigmoid(x)


def _mod_kernel(cc_ref, w_ref, b_ref, o_ref):
    a = _silu(cc_ref[...])
    o_ref[0] = jnp.dot(a.astype(BF16), w_ref[0].astype(BF16), preferred_element_type=F32) + b_ref[0]


def _modulation(c, c_ctx, mod_w, mod_b):
    depth, d, n = mod_w.shape
    bsz = c.shape[0]
    cc = jnp.concatenate([c.astype(F32), c_ctx.astype(F32)[None],
                          jnp.zeros((MOD_ROWS - bsz - 1, d), F32)], axis=0)
    tn = 1536
    out = pl.pallas_call(
        _mod_kernel,
        grid=(depth, n // tn),
        in_specs=[pl.BlockSpec((MOD_ROWS, d), lambda i, j: (0, 0)),
                  pl.BlockSpec((1, d, tn), lambda i, j: (i, 0, j)),
                  pl.BlockSpec((1, 1, tn), lambda i, j: (i, 0, j))],
        out_specs=pl.BlockSpec((1, MOD_ROWS, tn), lambda i, j: (i, 0, j)),
        out_shape=jax.ShapeDtypeStruct((depth, MOD_ROWS, n), F32),
        compiler_params=pltpu.CompilerParams(dimension_semantics=("parallel", "parallel"),
                                             vmem_limit_bytes=VMEM_LIMIT),
        name="adaln_mod",
    )(cc, mod_w, mod_b.reshape(depth, 1, n))
    return out.reshape(depth, MOD_ROWS, N_MOD, d)


def _load_rows(x_ref, colmajor, kcols):
    if not colmajor:
        return [x_ref[0]]
    return [x_ref[0, :, j * D_MODEL:(j + 1) * D_MODEL] for j in range(kcols)]


def _x_spec(colmajor, tm, length):
    if colmajor:
        rows = length // GRID_W
        return pl.BlockSpec((1, rows, (tm // rows) * D_MODEL), lambda b, i: (b, 0, i))
    return pl.BlockSpec((1, tm, D_MODEL), lambda b, i: (b, i, 0))


def _x_view(x, colmajor):
    bsz, length, d = x.shape
    if colmajor:
        return x.reshape(bsz, length // GRID_W, GRID_W * d)
    return x


def _norm_kernel(x_ref, m_ref, g_ref, o_ref, *, colmajor, kcols):
    pieces = _load_rows(x_ref, colmajor, kcols)
    rows = pieces[0].shape[0]
    for j, x in enumerate(pieces):
        h = _rms(x, g_ref[...]) * (1.0 + m_ref[0, 1:2, :]) + m_ref[0, 0:1, :]
        o_ref[0, j * rows:(j + 1) * rows, :] = h.astype(o_ref.dtype)


def _norm_mod(x, mod, g, *, colmajor, tm, out_dtype=F32):
    bsz, length, d = x.shape
    kcols = tm // (length // GRID_W)
    return pl.pallas_call(
        functools.partial(_norm_kernel, colmajor=colmajor, kcols=kcols),
        grid=(bsz, length // tm),
        in_specs=[_x_spec(colmajor, tm, length),
                  pl.BlockSpec((1, N_MOD, d), lambda b, i: (b, 0, 0)),
                  pl.BlockSpec((1, d), lambda b, i: (0, 0))],
        out_specs=pl.BlockSpec((1, tm, d), lambda b, i: (b, i, 0)),
        out_shape=jax.ShapeDtypeStruct((bsz, length, d), out_dtype),
        compiler_params=pltpu.CompilerParams(dimension_semantics=("parallel", "parallel"),
                                             vmem_limit_bytes=VMEM_LIMIT),
        name="norm_mod",
    )(_x_view(x, colmajor), mod, g.reshape(1, d))


def _ffn_kernel(x_ref, y_ref, m_ref, g_ref, win_ref, wout_ref, o_ref, *, colmajor, kcols):
    xs = _load_rows(x_ref, colmajor, kcols)
    rows = xs[0].shape[0]
    for j, x in enumerate(xs):
        y = y_ref[0, j * rows:(j + 1) * rows, :].astype(F32)
        lat = x + m_ref[0, 2:3, :] * _rms(y, g_ref[1:2, :])
        h = (_rms(lat, g_ref[2:3, :]) * (1.0 + m_ref[0, 4:5, :]) + m_ref[0, 3:4, :]).astype(BF16)

        def body(c, acc):
            gate = jnp.dot(h, win_ref[0, c], preferred_element_type=F32)
            up = jnp.dot(h, win_ref[1, c], preferred_element_type=F32)
            a = (_silu(gate) * up).astype(BF16)
            return acc + jnp.dot(a, wout_ref[c], preferred_element_type=F32)

        acc = lax.fori_loop(0, N_FFN_CHUNKS, body, jnp.zeros((rows, D_MODEL), F32))
        out = lat + m_ref[0, 5:6, :] * _rms(acc, g_ref[3:4, :])
        if colmajor:
            o_ref[0, :, j * D_MODEL:(j + 1) * D_MODEL] = out
        else:
            o_ref[0] = out


def _post_ffn(x, y, mod, g4, win, wout, *, colmajor, tm):
    bsz, length, d = x.shape
    kcols = tm // (length // GRID_W)
    xv = _x_view(x, colmajor)
    out = pl.pallas_call(
        functools.partial(_ffn_kernel, colmajor=colmajor, kcols=kcols),
        grid=(bsz, length // tm),
        in_specs=[_x_spec(colmajor, tm, length),
                  pl.BlockSpec((1, tm, d), lambda b, i: (b, i, 0)),
                  pl.BlockSpec((1, N_MOD, d), lambda b, i: (b, 0, 0)),
                  pl.BlockSpec((4, d), lambda b, i: (0, 0)),
                  pl.BlockSpec((2, N_FFN_CHUNKS, d, FFN_CHUNK), lambda b, i: (0, 0, 0, 0)),
                  pl.BlockSpec((N_FFN_CHUNKS, FFN_CHUNK, d), lambda b, i: (0, 0, 0))],
        out_specs=_x_spec(colmajor, tm, length),
        out_shape=jax.ShapeDtypeStruct(xv.shape, F32),
        compiler_params=pltpu.CompilerParams(dimension_semantics=("parallel", "parallel"),
                                             vmem_limit_bytes=VMEM_LIMIT),
        name="post_ffn",
    )(xv, y, mod, g4, win, wout)
    return out.reshape(bsz, length, d)


def _ffn_weights(w_in, w_out):
    d = w_in.shape[0]
    win = w_in.astype(BF16).reshape(d, 2, N_FFN_CHUNKS, FFN_CHUNK).transpose(1, 2, 0, 3)
    wout = w_out.astype(BF16).reshape(N_FFN_CHUNKS, FFN_CHUNK, d)
    return win, wout


def flip_seq(t):
    return jnp.flip(t, axis=1)


def head_rms_norm(o, g):
    bsz, length, heads, d = o.shape
    o = o * lax.rsqrt(jnp.mean(o * o, axis=-1, keepdims=True) + EPS) * g
    return o.reshape(bsz, length, heads * d)


def multihead_layer_norm(o, g):
    bsz, length, heads, d = o.shape
    o = o - jnp.mean(o, axis=-1, keepdims=True)
    o = o * lax.rsqrt(jnp.mean(o * o, axis=-1, keepdims=True) + EPS)
    return o.reshape(bsz, length, heads * d) * g


def depthwise_conv_centred(x, w, b):
    k = w.shape[0]
    y = lax.conv_general_dilated(x, w.astype(x.dtype)[:, None, :], window_strides=(1,),
                                 padding=[(k // 2, k // 2)], dimension_numbers=('NWC', 'WIO', 'NWC'),
                                 feature_group_count=x.shape[-1])
    return y + b


def gla_chunk_scan(q, k, v, g, s0):
    bsz, length, heads, _ = q.shape
    dv = v.shape[-1]
    n_chunks = length // GLA_CHUNK

    def chunked(t):
        return t.reshape(bsz, n_chunks, GLA_CHUNK, heads, t.shape[-1]).swapaxes(0, 1)

    causal = jnp.tril(jnp.ones((GLA_CHUNK, GLA_CHUNK), dtype=bool))

    def step(s, inp):
        qc, kc, vc, gc = inp
        b = jnp.cumsum(gc, axis=1)
        b_end = b[:, -1]
        q_dec = qc * jnp.exp(b)
        scores = jnp.einsum('bihd,bjhd->bhij', q_dec, kc * jnp.exp(-b))
        scores = jnp.where(causal, scores, 0.0)
        o = jnp.einsum('bhij,bjhv->bihv', scores, vc) + jnp.einsum('bihd,bhdv->bihv', q_dec, s)
        k_end = kc * jnp.exp(b_end[:, None] - b)
        s = jnp.exp(b_end)[..., None] * s + jnp.einsum('bjhd,bjhv->bhdv', k_end, vc)
        return s, o

    s_end, o = lax.scan(step, s0, (chunked(q), chunked(k), chunked(v), chunked(g)))
    return o.swapaxes(0, 1).reshape(bsz, length, heads, dv), s_end


def gla_mixer(h_lat, h_ctx, w_in, w_gate, b_gate, norm_g, w_out):
    def project(h):
        bsz, length, _ = h.shape
        z = h.astype(jnp.float32) @ w_in
        q, k, v, r, glr = jnp.split(z, [GLA_QK, 2 * GLA_QK, 2 * GLA_QK + GLA_VD,
                                       2 * GLA_QK + GLA_VD + D_MODEL], axis=-1)
        glr = glr.reshape(bsz, length, 2, GLA_RANK)
        g = jax.nn.log_sigmoid(jnp.einsum('blzr,zrk->blzk', glr, w_gate) + b_gate) / GLA_GATE_NORM
        g = g.reshape(bsz, length, 2, GLA_HEADS, GLA_DK)
        q = q.reshape(bsz, length, GLA_HEADS, GLA_DK) * GLA_DK ** -0.5
        k = k.reshape(bsz, length, GLA_HEADS, GLA_DK)
        v = v.reshape(bsz, length, GLA_HEADS, GLA_DV)
        return q, k, v, r, g[:, :, 0], g[:, :, 1]

    def bidirectional(h, s_f0, s_b0):
        q, k, v, r, g_f, g_b = project(h)
        o_f, s_f = gla_chunk_scan(q, k, v, g_f, s_f0)
        o_b, s_b = gla_chunk_scan(flip_seq(q), flip_seq(k), flip_seq(v), flip_seq(g_b), s_b0)
        o = head_rms_norm(o_f + flip_seq(o_b), norm_g)
        return (o * jax.nn.silu(r)) @ w_out, s_f, s_b

    s0 = jnp.zeros((h_ctx.shape[0], GLA_HEADS, GLA_DK, GLA_DV), jnp.float32)
    y_ctx, s_f, s_b = bidirectional(h_ctx, s0, s0)
    y_lat, _, _ = bidirectional(h_lat, s_f, s_b)
    return y_lat, y_ctx


def hyena_filter(length, w1, b1, w2, b2, w3, freq):
    pos = jnp.arange(length, dtype=jnp.float32)
    t = pos / length
    bands = jnp.linspace(1e-4, HY_BANDS - 1, HY_BANDS, dtype=jnp.float32)
    ang = 2.0 * math.pi * t[:, None] * bands[None]
    feats = jnp.concatenate([t[:, None], jnp.cos(ang), -jnp.sin(ang)], axis=-1)
    z = jnp.sin(freq * (feats @ w1 + b1))
    z = jnp.sin(freq * (z @ w2 + b2))
    h = (z @ w3).reshape(length, 2, D_MODEL)
    log_target = math.log(HY_DECAY_TARGET)
    deltas = jnp.abs(jnp.linspace(log_target / HY_SLOW_DECAY, log_target / HY_FAST_DECAY, D_MODEL,
                                  dtype=jnp.float32))
    decay = jnp.exp(-t[:, None] * deltas)
    h = h * decay[:, None]
    return h[:, 0], h[:, 1]


def fft_long_conv(u, h_f, h_b, bias):
    length = u.shape[1]
    h2 = jnp.concatenate([h_f[:1] + h_b[:1], h_f[1:], jnp.zeros_like(h_f[:1]),
                          jnp.flip(h_b[1:], axis=0)], axis=0)
    u_hat = jnp.fft.rfft(u, n=2 * length, axis=1)
    h_hat = jnp.fft.rfft(h2, axis=0)
    y = jnp.fft.irfft(u_hat * h_hat[None], n=2 * length, axis=1)[:, :length]
    return y + u * bias


def hyena_mixer(h_lat, h_ctx, w_in, conv_w, conv_b, f_w1, f_b1, f_w2, f_b2, f_w3, f_freq, f_bias, w_out):
    def run(h):
        length = h.shape[1]
        u = depthwise_conv_centred(h.astype(jnp.float32) @ w_in, conv_w, conv_b)
        x0, x1, v = jnp.split(u, 3, axis=-1)
        h_f, h_b = hyena_filter(length, f_w1, f_b1, f_w2, f_b2, f_w3, f_freq)
        y = x0 * fft_long_conv(v * x1, h_f, h_b, f_bias)
        return y @ w_out

    return run(h_lat), run(h_ctx)


def mlstm_chunk_scan(q, k, v, i_pre, log_f, state):
    bsz, length, heads, _ = q.shape
    n_chunks = length // ML_CHUNK

    def chunked(t):
        return t.reshape((bsz, n_chunks, ML_CHUNK) + t.shape[2:]).swapaxes(0, 1)

    causal = jnp.tril(jnp.ones((ML_CHUNK, ML_CHUNK), dtype=bool))[None, :, :, None]

    def step(carry, inp):
        c_mat, n_vec, m = carry
        qc, kc, vc, ic, fc = inp
        b = jnp.cumsum(fc, axis=1)
        b_end = b[:, -1]
        d_intra = jnp.where(causal, b[:, :, None] - b[:, None] + ic[:, None], -jnp.inf)
        d_inter = b + m[:, None]
        m_tok = jnp.maximum(jnp.max(d_intra, axis=2), d_inter)
        w_inter = jnp.exp(d_inter - m_tok)
        s = jnp.einsum('bihd,bjhd->bijh', qc, kc) * jnp.exp(d_intra - m_tok[:, :, None])
        num = jnp.einsum('bijh,bjhv->bihv', s, vc) + w_inter[..., None] * jnp.einsum('bihd,bhdv->bihv', qc, c_mat)
        den = jnp.sum(s, axis=2) + w_inter * jnp.einsum('bihd,bhd->bih', qc, n_vec)
        h = num / jnp.maximum(jnp.abs(den), jnp.exp(-m_tok))[..., None]
        d_state = b_end[:, None] - b + ic
        m_new = jnp.maximum(b_end + m, jnp.max(d_state, axis=1))
        w_prev = jnp.exp(b_end + m - m_new)
        w_tok = jnp.exp(d_state - m_new[:, None])
        c_mat = w_prev[..., None, None] * c_mat + jnp.einsum('bjh,bjhd,bjhv->bhdv', w_tok, kc, vc)
        n_vec = w_prev[..., None] * n_vec + jnp.einsum('bjh,bjhd->bhd', w_tok, kc)
        return (c_mat, n_vec, m_new), h

    state, h = lax.scan(step, state, (chunked(q), chunked(k), chunked(v), chunked(i_pre), chunked(log_f)))
    return h.swapaxes(0, 1).reshape(bsz, length, heads, v.shape[-1]), state


def mlstm_mixer(h_lat, h_ctx, w_in, conv_w, conv_b, w_q, w_k, w_v, w_gates, b_gates, norm_g, skip, w_out):
    def project(h):
        bsz, length, _ = h.shape
        x_m, z = jnp.split(h.astype(jnp.float32) @ w_in, 2, axis=-1)
        x_c = jax.nn.silu(depthwise_conv_centred(x_m, conv_w, conv_b))

        def block_diag(t, w):
            t = t.reshape(bsz, length, ML_NBLOCKS, ML_BLOCK)
            return jnp.einsum('blnc,ncd->blnd', t, w).reshape(bsz, length, ML_INNER)

        q, k, v = block_diag(x_c, w_q), block_diag(x_c, w_k), block_diag(x_m, w_v)
        gates = jnp.einsum('blc,zcg->blzg', jnp.concatenate([q, k, v], axis=-1), w_gates) + b_gates
        i_pre = gates[..., :ML_HEADS]
        log_f = jax.nn.log_sigmoid(gates[..., ML_HEADS:])
        heads = lambda t: t.reshape(bsz, length, ML_HEADS, ML_DH)
        return heads(q), heads(k) * ML_DH ** -0.5, heads(v), i_pre, log_f, x_c, z

    def bidirectional(h, state_f, state_b):
        q, k, v, i_pre, log_f, x_c, z = project(h)
        h_f, state_f = mlstm_chunk_scan(q, k, v, i_pre[:, :, 0], log_f[:, :, 0], state_f)
        h_b, state_b = mlstm_chunk_scan(flip_seq(q), flip_seq(k), flip_seq(v), flip_seq(i_pre[:, :, 1]),
                                        flip_seq(log_f[:, :, 1]), state_b)
        h_n = multihead_layer_norm(h_f + flip_seq(h_b), norm_g)
        return ((h_n + skip * x_c) * jax.nn.silu(z)) @ w_out, state_f, state_b

    bsz = h_ctx.shape[0]
    state0 = (jnp.zeros((bsz, ML_HEADS, ML_DH, ML_DH), jnp.float32),
              jnp.zeros((bsz, ML_HEADS, ML_DH), jnp.float32),
              jnp.full((bsz, ML_HEADS), -jnp.inf, jnp.float32))
    y_ctx, st_f, st_b = bidirectional(h_ctx, state0, state0)
    y_lat, _, _ = bidirectional(h_lat, st_f, st_b)
    return y_lat, y_ctx


def _linear_recurrence_combine(left, right):
    a_l, b_l = left
    a_r, b_r = right
    return a_l * a_r, a_r * b_l + b_r


def s5_mixer(h_lat, h_ctx, lam_re, lam_im, log_dt, b_re, b_im, c_re, c_im, d_skip, w_glu):
    f32 = jnp.float32
    b_mat = lax.complex(b_re.astype(f32), b_im.astype(f32))
    c_mat = lax.complex(c_re.astype(f32), c_im.astype(f32))

    def discretise(direction):
        lam = lax.complex(lam_re[direction].astype(f32), lam_im[direction].astype(f32))
        lam_bar = jnp.exp(lam * jnp.exp(log_dt[direction].astype(f32))[:, None])
        return lam_bar, ((lam_bar - 1.0) / lam)[..., None] * b_mat

    disc = (discretise(0), discretise(1))

    def scan_readout(u_seq, lam_bar, b_bar, x0):
        bu = jnp.einsum('gpc,lbgc->lbgp', b_bar, u_seq)
        a = jnp.broadcast_to(lam_bar, (u_seq.shape[0], 1) + lam_bar.shape)
        a_cum, xs = lax.associative_scan(_linear_recurrence_combine, (a, bu), axis=0)
        xs = xs + a_cum * x0
        return jnp.real(jnp.einsum('gcp,lbgp->lbgc', c_mat, xs)), xs[-1]

    def bidirectional(h, x0_f, x0_b):
        bsz, length, _ = h.shape
        u = h.astype(f32)
        u_seq = u.swapaxes(0, 1).reshape(length, bsz, S5_GROUPS, S5_GROUP).astype(jnp.complex64)
        y_f, x_f = scan_readout(u_seq, disc[0][0], disc[0][1], x0_f)
        y_b, x_b = scan_readout(jnp.flip(u_seq, axis=0), disc[1][0], disc[1][1], x0_b)
        y = (y_f + jnp.flip(y_b, axis=0)).reshape(length, bsz, D_MODEL).swapaxes(0, 1) + d_skip * u
        val, gate = jnp.split(jax.nn.gelu(y) @ w_glu, 2, axis=-1)
        return val * jax.nn.sigmoid(gate), x_f, x_b

    x0 = jnp.zeros((h_ctx.shape[0], S5_GROUPS, S5_STATE), jnp.complex64)
    y_ctx, x_f, x_b = bidirectional(h_ctx, x0, x0)
    y_lat, _, _ = bidirectional(h_lat, x_f, x_b)
    return y_lat, y_ctx


def kernel(x, c, ctx, c_ctx, mod_w, mod_b, norm_g, ffn_w_in, ffn_w_out, gla_w_in, gla_w_gate, gla_b_gate, gla_norm_g, gla_w_out, hy_w_in, hy_conv_w, hy_conv_b, hy_f_w1, hy_f_b1, hy_f_w2, hy_f_b2, hy_f_w3, hy_f_freq, hy_f_bias, hy_w_out, ml_w_in, ml_conv_w, ml_conv_b, ml_w_q, ml_w_k, ml_w_v, ml_w_gates, ml_b_gates, ml_norm_g, ml_skip, ml_w_out, s5_lam_re, s5_lam_im, s5_log_dt, s5_b_re, s5_b_im, s5_c_re, s5_c_im, s5_d, s5_w_glu):
    bsz, seq, d = x.shape
    ctx_len = ctx.shape[1]
    depth = mod_w.shape[0]
    mods = _modulation(c, c_ctx, mod_w, mod_b)
    lat = x.astype(F32)
    cx = ctx.astype(F32).reshape(1, bsz * ctx_len, d)
    tm_lat, tm_ctx = 256, 256
    for i in range(depth):
        kind, j = i % N_MIXERS, i // N_MIXERS
        last = i == depth - 1
        colmajor = kind >= 2
        mod_l = mods[i, :bsz]
        mod_c = mods[i, bsz:bsz + 1]
        h_lat = _norm_mod(lat, mod_l, norm_g[i, 0], colmajor=colmajor, tm=tm_lat)
        h_ctx = _norm_mod(cx, mod_c, norm_g[i, 0], colmajor=False, tm=tm_ctx).reshape(bsz, ctx_len, d)
        if kind == 0:
            y_lat, y_ctx = gla_mixer(h_lat, h_ctx, gla_w_in[j], gla_w_gate[j], gla_b_gate[j], gla_norm_g[j],
                                     gla_w_out[j])
        elif kind == 1:
            y_lat, y_ctx = hyena_mixer(h_lat, h_ctx, hy_w_in[j], hy_conv_w[j], hy_conv_b[j], hy_f_w1[j],
                                       hy_f_b1[j], hy_f_w2[j], hy_f_b2[j], hy_f_w3[j], hy_f_freq[j],
                                       hy_f_bias[j], hy_w_out[j])
        elif kind == 2:
            y_lat, y_ctx = mlstm_mixer(h_lat, h_ctx, ml_w_in[j], ml_conv_w[j], ml_conv_b[j],
                                       ml_w_q[j], ml_w_k[j], ml_w_v[j], ml_w_gates[j], ml_b_gates[j],
                                       ml_norm_g[j], ml_skip[j], ml_w_out[j])
        else:
            y_lat, y_ctx = s5_mixer(h_lat, h_ctx, s5_lam_re[j], s5_lam_im[j], s5_log_dt[j],
                                    s5_b_re[j], s5_b_im[j], s5_c_re[j], s5_c_im[j], s5_d[j], s5_w_glu[j])
        win, wout = _ffn_weights(ffn_w_in[i], ffn_w_out[i])
        lat = _post_ffn(lat, y_lat, mod_l, norm_g[i], win, wout, colmajor=colmajor, tm=tm_lat)
        if not last:
            cx = _post_ffn(cx, y_ctx.reshape(1, bsz * ctx_len, d), mod_c, norm_g[i], win, wout,
                           colmajor=False, tm=tm_ctx)
    return lat
```

```python
import functools
import math

import jax
import jax.numpy as jnp
from jax import lax
from jax.experimental import pallas as pl
from jax.experimental.pallas import tpu as pltpu

F32 = jnp.float32
BF16 = jnp.bfloat16

D_MODEL = 1024
GRID_W = 64
N_MOD = 6
EPS = 1e-6
FFN_HIDDEN = 2816
FFN_CHUNK = 256
N_FFN_CHUNKS = FFN_HIDDEN // FFN_CHUNK
MOD_ROWS = 16
VMEM_LIMIT = 56 * 1024 * 1024
LANES = 128
N_MIXERS = 4
PROJ_TN = 512

GLA_HEADS = 4
GLA_QK = D_MODEL // 2
GLA_VD = D_MODEL
GLA_DK = GLA_QK // GLA_HEADS
GLA_DV = GLA_VD // GLA_HEADS
GLA_RANK = 16
GLA_GATE_NORM = 16.0
GLA_CHUNK = 64

HALO = 16
HY_BANDS = 16
HY_DECAY_TARGET = 1e-2
HY_FAST_DECAY = 0.3
HY_SLOW_DECAY = 1.5
HY_CT = 256
HY_N1, HY_N2 = 64, 128
HY_TC = 8192

ML_HEADS = 4
ML_INNER = 2 * D_MODEL
ML_DH = ML_INNER // ML_HEADS
ML_BLOCK = 4
ML_CHUNK = 64
ML_CT = 256

S5_GROUP = 16
S5_GROUPS = D_MODEL // S5_GROUP
S5_STATE = 64
S5_TILE_G = LANES // S5_GROUP
S5_NT = S5_GROUPS // S5_TILE_G
S5_SL = S5_TILE_G * S5_STATE
S5_TB = 64


def _rms(x, g):
    return x * lax.rsqrt(jnp.mean(x * x, axis=-1, keepdims=True) + EPS) * g


def _silu(x):
    return x * jax.nn.sigmoid(x)


def _params(*sem):
    return pltpu.CompilerParams(dimension_semantics=sem, vmem_limit_bytes=VMEM_LIMIT)


def _mod_kernel(cc_ref, w_ref, b_ref, o_ref):
    a = _silu(cc_ref[...])
    o_ref[0] = jnp.dot(a.astype(BF16), w_ref[0].astype(BF16), preferred_element_type=F32) + b_ref[0]


def _modulation(c, c_ctx, mod_w, mod_b):
    depth, d, n = mod_w.shape
    bsz = c.shape[0]
    cc = jnp.concatenate([c.astype(F32), c_ctx.astype(F32)[None],
                          jnp.zeros((MOD_ROWS - bsz - 1, d), F32)], axis=0)
    tn = 1536
    out = pl.pallas_call(
        _mod_kernel,
        grid=(depth, n // tn),
        in_specs=[pl.BlockSpec((MOD_ROWS, d), lambda i, j: (0, 0)),
                  pl.BlockSpec((1, d, tn), lambda i, j: (i, 0, j)),
                  pl.BlockSpec((1, 1, tn), lambda i, j: (i, 0, j))],
        out_specs=pl.BlockSpec((1, MOD_ROWS, tn), lambda i, j: (i, 0, j)),
        out_shape=jax.ShapeDtypeStruct((depth, MOD_ROWS, n), F32),
        compiler_params=_params("parallel", "parallel"),
        name="adaln_mod",
    )(cc, mod_w, mod_b.reshape(depth, 1, n))
    return out.reshape(depth, MOD_ROWS, N_MOD, d)


def _load_rows(x_ref, colmajor, kcols):
    if not colmajor:
        return x_ref[0]
    return jnp.concatenate([x_ref[0, :, j * D_MODEL:(j + 1) * D_MODEL] for j in range(kcols)], axis=0)


def _store_rows(o_ref, val, colmajor, kcols):
    if not colmajor:
        o_ref[0] = val
        return
    rows = val.shape[0] // kcols
    for j in range(kcols):
        o_ref[0, :, j * D_MODEL:(j + 1) * D_MODEL] = val[j * rows:(j + 1) * rows]


def _x_spec(colmajor, tm, length):
    if colmajor:
        rows = length // GRID_W
        return pl.BlockSpec((1, rows, (tm // rows) * D_MODEL), lambda b, i: (b, 0, i))
    return pl.BlockSpec((1, tm, D_MODEL), lambda b, i: (b, i, 0))


def _x_view(x, colmajor):
    bsz, length, d = x.shape
    if colmajor:
        return x.reshape(bsz, length // GRID_W, GRID_W * d)
    return x


def _seq_spec(tmajor, tm, width):
    if tmajor:
        return pl.BlockSpec((1, tm, width), lambda b, i: (0, i, b))
    return pl.BlockSpec((1, tm, width), lambda b, i: (b, i, 0))


def _mod_spec(mod):
    if mod.shape[0] > 1:
        return pl.BlockSpec((1, N_MOD, D_MODEL), lambda b, i: (b, 0, 0))
    return pl.BlockSpec((1, N_MOD, D_MODEL), lambda b, i: (0, 0, 0))


def _norm_kernel(x_ref, m_ref, g_ref, o_ref, *, colmajor, kcols):
    x = _load_rows(x_ref, colmajor, kcols)
    h = _rms(x, g_ref[...]) * (1.0 + m_ref[0, 1:2, :]) + m_ref[0, 0:1, :]
    o_ref[0] = h.astype(o_ref.dtype)


def _norm_mod(x, mod, g, *, colmajor, tm, tmajor=False, out_dtype=F32):
    bsz, length, d = x.shape
    kcols = tm // (length // GRID_W)
    out_shape = (1, length, bsz * d) if tmajor else (bsz, length, d)
    return pl.pallas_call(
        functools.partial(_norm_kernel, colmajor=colmajor, kcols=kcols),
        grid=(bsz, length // tm),
        in_specs=[_x_spec(colmajor, tm, length), _mod_spec(mod),
                  pl.BlockSpec((1, d), lambda b, i: (0, 0))],
        out_specs=_seq_spec(tmajor, tm, d),
        out_shape=jax.ShapeDtypeStruct(out_shape, out_dtype),
        compiler_params=_params("parallel", "parallel"),
        name="norm_mod",
    )(_x_view(x, colmajor), mod, g.reshape(1, d))


def _proj_kernel(*refs, colmajor, kcols, n_w):
    x_ref, m_ref, g_ref = refs[:3]
    w_refs, o_refs = refs[3:3 + n_w], refs[3 + n_w:]
    x = _load_rows(x_ref, colmajor, kcols)
    h = (_rms(x, g_ref[...]) * (1.0 + m_ref[0, 1:2, :]) + m_ref[0, 0:1, :]).astype(BF16)
    for w_ref, o_ref in zip(w_refs, o_refs):
        n = w_ref.shape[1]
        tn = min(n, PROJ_TN)
        for n0 in range(0, n, tn):
            o_ref[0, :, n0:n0 + tn] = jnp.dot(h, w_ref[:, n0:n0 + tn],
                                              preferred_element_type=F32).astype(o_ref.dtype)


def _norm_proj(x, mod, g, ws, dtypes, *, colmajor, tm):
    bsz, length, d = x.shape
    kcols = tm // (length // GRID_W)
    full = lambda shape: pl.BlockSpec(shape, lambda b, i: (0,) * len(shape))
    return pl.pallas_call(
        functools.partial(_proj_kernel, colmajor=colmajor, kcols=kcols, n_w=len(ws)),
        grid=(bsz, length // tm),
        in_specs=[_x_spec(colmajor, tm, length), _mod_spec(mod), full((1, d))] + [full(w.shape) for w in ws],
        out_specs=[pl.BlockSpec((1, tm, w.shape[1]), lambda b, i: (b, i, 0)) for w in ws],
        out_shape=[jax.ShapeDtypeStruct((bsz, length, w.shape[1]), dt) for w, dt in zip(ws, dtypes)],
        compiler_params=_params("parallel", "parallel"),
        name="norm_proj",
    )(_x_view(x, colmajor), mod, g.reshape(1, d), *ws)


def _ffn_kernel(*refs, colmajor, kcols, has_wmix):
    if has_wmix:
        x_ref, y_ref, m_ref, g_ref, wmix_ref, win_ref, wout_ref, o_ref = refs
    else:
        x_ref, y_ref, m_ref, g_ref, win_ref, wout_ref, o_ref = refs
    x = _load_rows(x_ref, colmajor, kcols)
    if has_wmix:
        y = jnp.dot(y_ref[0], wmix_ref[...], preferred_element_type=F32)
    else:
        y = y_ref[0].astype(F32)
    lat = x + m_ref[0, 2:3, :] * _rms(y, g_ref[1:2, :])
    h = (_rms(lat, g_ref[2:3, :]) * (1.0 + m_ref[0, 4:5, :]) + m_ref[0, 3:4, :]).astype(BF16)

    def body(c, acc):
        gate = jnp.dot(h, win_ref[0, c], preferred_element_type=F32)
        up = jnp.dot(h, win_ref[1, c], preferred_element_type=F32)
        a = (_silu(gate) * up).astype(BF16)
        return acc + jnp.dot(a, wout_ref[c], preferred_element_type=F32)

    acc = lax.fori_loop(0, N_FFN_CHUNKS, body, jnp.zeros(x.shape, F32))
    out = lat + m_ref[0, 5:6, :] * _rms(acc, g_ref[3:4, :])
    _store_rows(o_ref, out, colmajor, kcols)


def _post_ffn(x, y, mod, g4, wmix, win, wout, *, colmajor, tm, tmajor=False):
    bsz, length, d = x.shape
    kcols = tm // (length // GRID_W)
    xv = _x_view(x, colmajor)
    width = y.shape[-1] // bsz if tmajor else y.shape[-1]
    full = lambda shape: pl.BlockSpec(shape, lambda b, i: (0,) * len(shape))
    in_specs = [_x_spec(colmajor, tm, length), _seq_spec(tmajor, tm, width), _mod_spec(mod), full((4, d))]
    args = [xv, y, mod, g4]
    if wmix is not None:
        in_specs.append(full(wmix.shape))
        args.append(wmix)
    in_specs += [full(win.shape), full(wout.shape)]
    args += [win, wout]
    out = pl.pallas_call(
        functools.partial(_ffn_kernel, colmajor=colmajor, kcols=kcols, has_wmix=wmix is not None),
        grid=(bsz, length // tm),
        in_specs=in_specs,
        out_specs=_x_spec(colmajor, tm, length),
        out_shape=jax.ShapeDtypeStruct(xv.shape, F32),
        compiler_params=_params("parallel", "parallel"),
        name="post_ffn",
    )(*args)
    return out.reshape(bsz, length, d)


def _ffn_weights(w_in, w_out):
    d = w_in.shape[0]
    win = w_in.astype(BF16).reshape(d, 2, N_FFN_CHUNKS, FFN_CHUNK).transpose(1, 2, 0, 3)
    wout = w_out.astype(BF16).reshape(N_FFN_CHUNKS, FFN_CHUNK, d)
    return win, wout


def _s5_disc_kernel(lre_ref, lim_ref, ldt_ref, bre_ref, bim_ref, lbr_ref, lbi_ref, bbr_ref, bbi_ref):
    lre, lim = lre_ref[...], lim_ref[...]
    dt = jnp.exp(ldt_ref[...])
    mag = jnp.exp(lre * dt)
    lbr = mag * jnp.cos(lim * dt)
    lbi = mag * jnp.sin(lim * dt)
    lbr_ref[...] = lbr
    lbi_ref[...] = lbi
    nr = lbr - 1.0
    den = lre * lre + lim * lim
    cr = (nr * lre + lbi * lim) / den
    ci = (lbi * lre - nr * lim) / den
    for dr in range(2):
        bbr_ref[dr] = cr[dr:dr + 1] * bre_ref[...] - ci[dr:dr + 1] * bim_ref[...]
        bbi_ref[dr] = cr[dr:dr + 1] * bim_ref[...] + ci[dr:dr + 1] * bre_ref[...]


def _s5_discretise(lam_re, lam_im, log_dt, b_re, b_im):
    gp = S5_GROUPS * S5_STATE
    flat = lambda t: t.astype(F32).reshape(2, gp)
    ldt = jnp.repeat(log_dt.astype(F32), S5_STATE, axis=1)
    bt = lambda t: t.astype(F32).reshape(gp, S5_GROUP).T
    shp = jax.ShapeDtypeStruct
    return pl.pallas_call(
        _s5_disc_kernel,
        out_shape=(shp((2, gp), F32), shp((2, gp), F32),
                   shp((2, S5_GROUP, gp), F32), shp((2, S5_GROUP, gp), F32)),
        name="s5_discretise",
    )(flat(lam_re), flat(lam_im), ldt, bt(b_re), bt(b_im))


def _s5_scan_kernel(u_ref, wb_ref, lam_ref, wc_ref, x0_ref, y_ref, xt_ref, bu_scr, st_scr, *, reverse, nb):
    i = pl.program_id(1)

    @pl.when(i == 0)
    def _():
        st_scr[...] = x0_ref[0]

    bu_scr[...] = jnp.dot(u_ref[0].astype(BF16), wb_ref[0], preferred_element_type=F32)
    lr = jnp.broadcast_to(lam_ref[0, 0], (nb, S5_SL))
    li = jnp.broadcast_to(lam_ref[0, 1], (nb, S5_SL))

    def step(s, carry):
        xr, xi = carry
        t = (S5_TB - 1 - s) if reverse else s
        r0 = pl.multiple_of(t * nb, nb)
        nxr = lr * xr - li * xi + bu_scr[pl.ds(r0, nb), 0:S5_SL]
        nxi = lr * xi + li * xr + bu_scr[pl.ds(r0, nb), S5_SL:2 * S5_SL]
        bu_scr[pl.ds(r0, nb), 0:S5_SL] = nxr
        bu_scr[pl.ds(r0, nb), S5_SL:2 * S5_SL] = nxi
        return nxr, nxi

    xr, xi = lax.fori_loop(0, S5_TB, step, (st_scr[0], st_scr[1]), unroll=8)
    st_scr[0] = xr
    st_scr[1] = xi
    y_ref[0] = jnp.dot(bu_scr[...].astype(BF16), wc_ref[0], preferred_element_type=F32)

    @pl.when(i == pl.num_programs(1) - 1)
    def _():
        xt_ref[0, 0] = xr
        xt_ref[0, 1] = xi


def _s5_scan(u, wb, lam, wc, x0, *, reverse, nb):
    _, rows, d = u.shape
    nt = rows // (S5_TB * nb)
    tix = (lambda i: nt - 1 - i) if reverse else (lambda i: i)
    blk = S5_TB * nb
    shp = jax.ShapeDtypeStruct
    return pl.pallas_call(
        functools.partial(_s5_scan_kernel, reverse=reverse, nb=nb),
        grid=(S5_NT, nt),
        in_specs=[pl.BlockSpec((1, blk, LANES), lambda k, i: (0, tix(i), k)),
                  pl.BlockSpec((1, LANES, 2 * S5_SL), lambda k, i: (k, 0, 0)),
                  pl.BlockSpec((1, 2, 1, S5_SL), lambda k, i: (k, 0, 0, 0)),
                  pl.BlockSpec((1, 2 * S5_SL, LANES), lambda k, i: (k, 0, 0)),
                  pl.BlockSpec((1, 2, nb, S5_SL), lambda k, i: (k, 0, 0, 0))],
        out_specs=[pl.BlockSpec((1, blk, LANES), lambda k, i: (0, tix(i), k)),
                   pl.BlockSpec((1, 2, nb, S5_SL), lambda k, i: (k, 0, 0, 0))],
        out_shape=(shp((1, rows, d), F32), shp((S5_NT, 2, nb, S5_SL), F32)),
        scratch_shapes=[pltpu.VMEM((blk, 2 * S5_SL), F32), pltpu.VMEM((2, nb, S5_SL), F32)],
        compiler_params=_params("parallel", "arbitrary"),
        name="s5_scan",
    )(u, wb, lam, wc, x0)


def _s5_out_kernel(yf_ref, yb_ref, u_ref, d_ref, w_ref, o_ref):
    y = yf_ref[0] + yb_ref[0] + d_ref[...] * u_ref[0]
    a = jax.nn.gelu(y).astype(BF16)
    val = jnp.dot(a, w_ref[:, 0:D_MODEL], preferred_element_type=F32)
    gate = jnp.dot(a, w_ref[:, D_MODEL:2 * D_MODEL], preferred_element_type=F32)
    o_ref[0] = (val * jax.nn.sigmoid(gate)).astype(o_ref.dtype)


def _s5_out(y_f, y_b, u, d_skip, w_glu, *, tm):
    _, rows, d = u.shape
    row_spec = pl.BlockSpec((1, tm, d), lambda i: (0, i, 0))
    return pl.pallas_call(
        _s5_out_kernel,
        grid=(rows // tm,),
        in_specs=[row_spec, row_spec, row_spec,
                  pl.BlockSpec((1, d), lambda i: (0, 0)),
                  pl.BlockSpec((d, 2 * d), lambda i: (0, 0))],
        out_specs=row_spec,
        out_shape=jax.ShapeDtypeStruct((1, rows, d), BF16),
        compiler_params=_params("parallel"),
        name="s5_out",
    )(y_f, y_b, u, d_skip.astype(F32).reshape(1, d), w_glu.astype(BF16))


def _s5_mixer(u_lat, u_ctx, nb, lam_re, lam_im, log_dt, b_re, b_im, c_re, c_im, d_skip, w_glu):
    lbr, lbi, bbr, bbi = _s5_discretise(lam_re, lam_im, log_dt, b_re, b_im)
    eye = jnp.eye(S5_TILE_G, dtype=F32)
    bb = jnp.stack([bbr, bbi], axis=1).reshape(2, 2, S5_GROUP, S5_NT, S5_TILE_G, S5_STATE)
    wb = jnp.einsum('dzcktp,ts->dktczsp', bb, eye).reshape(2, S5_NT, LANES, 2 * S5_SL).astype(BF16)
    cc = jnp.stack([c_re.astype(F32), -c_im.astype(F32)], axis=0).reshape(2, S5_NT, S5_TILE_G, S5_GROUP, S5_STATE)
    wc = jnp.einsum('zktcp,ts->kzsptc', cc, eye).reshape(S5_NT, 2 * S5_SL, LANES).astype(BF16)
    lam = jnp.stack([lbr, lbi], axis=1).reshape(2, 2, S5_NT, 1, S5_SL).transpose(0, 2, 1, 3, 4)
    x0 = jnp.zeros((S5_NT, 2, nb, S5_SL), F32)
    yc_f, x_f = _s5_scan(u_ctx, wb[0], lam[0], wc, x0, reverse=False, nb=nb)
    yc_b, x_b = _s5_scan(u_ctx, wb[1], lam[1], wc, x0, reverse=True, nb=nb)
    yl_f, _ = _s5_scan(u_lat, wb[0], lam[0], wc, x_f, reverse=False, nb=nb)
    yl_b, _ = _s5_scan(u_lat, wb[1], lam[1], wc, x_b, reverse=True, nb=nb)
    y_lat = _s5_out(yl_f, yl_b, u_lat, d_skip, w_glu, tm=512)
    y_ctx = _s5_out(yc_f, yc_b, u_ctx, d_skip, w_glu, tm=512)
    return y_lat, y_ctx


_NT_DIMS = (((1,), (1,)), ((), ()))
_TN_DIMS = (((0,), (0,)), ((), ()))


def _tri_mask(n, reverse):
    r = lax.broadcasted_iota(jnp.int32, (n, n), 0)
    c = lax.broadcasted_iota(jnp.int32, (n, n), 1)
    return (c >= r) if reverse else (c <= r)


def _cumsum_f32(mask_bf16, g):
    g1 = g.astype(BF16)
    r1 = g - g1.astype(F32)
    g2 = r1.astype(BF16)
    g3 = (r1 - g2.astype(F32)).astype(BF16)
    dot = lambda t: jnp.dot(mask_bf16, t, preferred_element_type=F32)
    return dot(g1) + dot(g2) + dot(g3)


def _gla_scan_kernel(q_ref, k_ref, v_ref, glr_ref, wg_ref, bg_ref, s0_ref, o_ref, s_ref, *, reverse):
    @pl.when(pl.program_id(1) == 0)
    def _():
        s_ref[...] = s0_ref[...]

    ch = q_ref.shape[1]
    mask = _tri_mask(ch, reverse)
    pre = jnp.dot(glr_ref[0].astype(BF16), wg_ref[...], preferred_element_type=F32) + bg_ref[...]
    g = jax.nn.log_sigmoid(pre) / GLA_GATE_NORM
    b = _cumsum_f32(mask.astype(BF16), g)
    end = 0 if reverse else ch - 1
    b_end = b[end:end + 1, :]
    q_dec = (q_ref[0].astype(F32) * GLA_DK ** -0.5 * jnp.exp(b)).astype(BF16)
    k = k_ref[0].astype(F32)
    k_dec = (k * jnp.exp(-b)).astype(BF16)
    k_end = (k * jnp.exp(b_end - b)).astype(BF16)
    decay_end = jnp.exp(b_end)
    for h in range(GLA_HEADS):
        sk = slice(h * GLA_DK, (h + 1) * GLA_DK)
        sv = slice(h * GLA_DV, (h + 1) * GLA_DV)
        v = v_ref[0, :, sv]
        st = s_ref[0, h]
        scores = lax.dot_general(q_dec[:, sk], k_dec[:, sk], _NT_DIMS, preferred_element_type=F32)
        scores = jnp.where(mask, scores, 0.0).astype(BF16)
        o = (jnp.dot(scores, v, preferred_element_type=F32)
             + lax.dot_general(q_dec[:, sk], st.astype(BF16), _NT_DIMS, preferred_element_type=F32))
        o_ref[0, :, sv] = o.astype(o_ref.dtype)
        s_ref[0, h] = decay_end[:, sk] * st + lax.dot_general(v, k_end[:, sk], _TN_DIMS,
                                                               preferred_element_type=F32)


def _gla_scan(z, glr, wg, bg, s0, *, reverse, ch):
    bsz, length, _ = z.shape
    nc = length // ch
    cix = (lambda i: nc - 1 - i) if reverse else (lambda i: i)
    st_spec = pl.BlockSpec((1, GLA_HEADS, GLA_DV, GLA_DK), lambda b, i: (b, 0, 0, 0))
    shp = jax.ShapeDtypeStruct
    return pl.pallas_call(
        functools.partial(_gla_scan_kernel, reverse=reverse),
        grid=(bsz, nc),
        in_specs=[pl.BlockSpec((1, ch, GLA_QK), lambda b, i: (b, cix(i), 0)),
                  pl.BlockSpec((1, ch, GLA_QK), lambda b, i: (b, cix(i), 1)),
                  pl.BlockSpec((1, ch, GLA_VD), lambda b, i: (b, cix(i), 1)),
                  pl.BlockSpec((1, ch, LANES), lambda b, i: (b, cix(i), 0)),
                  pl.BlockSpec((LANES, GLA_QK), lambda b, i: (0, 0)),
                  pl.BlockSpec((1, GLA_QK), lambda b, i: (0, 0)),
                  st_spec],
        out_specs=[pl.BlockSpec((1, ch, GLA_VD), lambda b, i: (b, cix(i), 0)), st_spec],
        out_shape=[shp((bsz, length, GLA_VD), BF16), shp(s0.shape, F32)],
        compiler_params=_params("parallel", "arbitrary"),
        name="gla_scan",
    )(z, z, z, glr, wg, bg, s0)


def _gla_out_kernel(of_ref, ob_ref, r_ref, ng_ref, o_ref):
    for h in range(GLA_HEADS):
        sv = slice(h * GLA_DV, (h + 1) * GLA_DV)
        o = of_ref[0, :, sv].astype(F32) + ob_ref[0, :, sv].astype(F32)
        o = _rms(o, ng_ref[...])
        o_ref[0, :, sv] = (o * _silu(r_ref[0, :, sv].astype(F32))).astype(o_ref.dtype)


def _gla_out(o_f, o_b, z, norm_g, *, tm):
    bsz, length, _ = o_f.shape
    seq = pl.BlockSpec((1, tm, GLA_VD), lambda b, i: (b, i, 0))
    return pl.pallas_call(
        _gla_out_kernel,
        grid=(bsz, length // tm),
        in_specs=[seq, seq, pl.BlockSpec((1, tm, D_MODEL), lambda b, i: (b, i, 2)),
                  pl.BlockSpec((1, GLA_DV), lambda b, i: (0, 0))],
        out_specs=seq,
        out_shape=jax.ShapeDtypeStruct((bsz, length, GLA_VD), BF16),
        compiler_params=_params("parallel", "parallel"),
        name="gla_out",
    )(o_f, o_b, z, norm_g.astype(F32).reshape(1, GLA_DV))


def _gla_weights(w_in, w_gate, b_gate):
    n_main = 2 * GLA_QK + GLA_VD + D_MODEL
    w_main = w_in[:, :n_main].astype(BF16)
    w_glr = jnp.pad(w_in[:, n_main:], ((0, 0), (0, LANES - 2 * GLA_RANK))).astype(BF16)
    wg = jnp.zeros((2, LANES, GLA_QK), F32)
    wg = wg.at[0, :GLA_RANK].set(w_gate[0].astype(F32)).at[1, GLA_RANK:2 * GLA_RANK].set(w_gate[1].astype(F32))
    return w_main, w_glr, wg.astype(BF16), b_gate.astype(F32).reshape(2, 1, GLA_QK)


def _gla_mixer(z_lat, glr_lat, z_ctx, glr_ctx, wg, bg, norm_g):
    bsz = z_lat.shape[0]
    s0 = jnp.zeros((bsz, GLA_HEADS, GLA_DV, GLA_DK), F32)

    def bidirectional(z, glr, s_f, s_b, tm):
        o_f, s_f = _gla_scan(z, glr, wg[0], bg[0], s_f, reverse=False, ch=GLA_CHUNK)
        o_b, s_b = _gla_scan(z, glr, wg[1], bg[1], s_b, reverse=True, ch=GLA_CHUNK)
        return _gla_out(o_f, o_b, z, norm_g, tm=tm), s_f, s_b

    t_ctx, s_f, s_b = bidirectional(z_ctx, glr_ctx, s0, s0, z_ctx.shape[1])
    t_lat, _, _ = bidirectional(z_lat, glr_lat, s_f, s_b, 512)
    return t_lat, t_ctx


def _dot3(a, b):
    a1 = a.astype(BF16)
    a2 = (a - a1.astype(F32)).astype(BF16)
    b1 = b.astype(BF16)
    b2 = (b - b1.astype(F32)).astype(BF16)
    dot = lambda s, t: jnp.dot(s, t, preferred_element_type=F32)
    return dot(a1, b1) + dot(a1, b2) + dot(a2, b1)


def _hy_filter_kernel(bands_ref, w10_ref, w1c_ref, w1s_ref, b1_ref, w2_ref, b2_ref, w3_ref, fr_ref, dl_ref,
                      ha_ref, hb_ref, *, length):
    tm = ha_ref.shape[0]
    n = (pl.program_id(0) * tm + lax.broadcasted_iota(jnp.int32, (tm, 1), 0)).astype(F32)

    def mlp(pos):
        t = pos * (1.0 / length)
        ang = (2.0 * math.pi * t) * bands_ref[...]
        pre = (t * w10_ref[...] + _dot3(jnp.cos(ang), w1c_ref[...]) - _dot3(jnp.sin(ang), w1s_ref[...])
               + b1_ref[...])
        z = jnp.sin(fr_ref[...] * pre)
        z = jnp.sin(fr_ref[...] * (_dot3(z, w2_ref[...]) + b2_ref[...]))
        return z, jnp.exp(-t * dl_ref[...])

    z, dec = mlp(n)
    h_f = _dot3(z, w3_ref[:, 0:D_MODEL]) * dec
    h_b0 = _dot3(z, w3_ref[:, D_MODEL:2 * D_MODEL]) * dec
    ha_ref[...] = (h_f + jnp.where(n == 0.0, h_b0, 0.0)).astype(ha_ref.dtype)
    zb, decb = mlp(length - n)
    h_b = _dot3(zb, w3_ref[:, D_MODEL:2 * D_MODEL]) * decb
    hb_ref[...] = jnp.where(n > 0.0, h_b, 0.0).astype(hb_ref.dtype)


def _hy_filter(length, w1, b1, w2, b2, w3, freq):
    f32 = lambda t: t.astype(F32)
    nb = HY_BANDS
    bands = jnp.pad(jnp.linspace(1e-4, nb - 1, nb, dtype=F32), (0, LANES - nb)).reshape(1, LANES)
    w1 = f32(w1)
    w1c = jnp.pad(w1[1:1 + nb], ((0, LANES - nb), (0, 0)))
    w1s = jnp.pad(w1[1 + nb:1 + 2 * nb], ((0, LANES - nb), (0, 0)))
    log_target = math.log(HY_DECAY_TARGET)
    deltas = jnp.abs(jnp.linspace(log_target / HY_SLOW_DECAY, log_target / HY_FAST_DECAY, D_MODEL, dtype=F32))
    tm = min(length, 512)
    args = [bands, w1[0:1], w1c, w1s, f32(b1).reshape(1, -1), f32(w2), f32(b2).reshape(1, -1), f32(w3),
            f32(freq).reshape(1, -1), deltas.reshape(1, D_MODEL)]
    out_spec = pl.BlockSpec((tm, D_MODEL), lambda i: (i, 0))
    return pl.pallas_call(
        functools.partial(_hy_filter_kernel, length=length),
        grid=(length // tm,),
        in_specs=[pl.BlockSpec(a.shape, lambda i: (0, 0)) for a in args],
        out_specs=[out_spec, out_spec],
        out_shape=[jax.ShapeDtypeStruct((length, D_MODEL), BF16)] * 2,
        compiler_params=_params("parallel"),
        name="hy_filter",
    )(*args)


def _conv3(x_ref, xp_ref, xn_ref, cw_ref, cb_ref, r, nr):
    tm = x_ref.shape[1]
    x = x_ref[0].astype(F32)
    prev_row = jnp.where(r > 0, xp_ref[0][HALO - 1:HALO, :].astype(F32), 0.0)
    next_row = jnp.where(r < nr - 1, xn_ref[0][0:1, :].astype(F32), 0.0)
    rows = lax.broadcasted_iota(jnp.int32, (tm, 1), 0)
    x_prev = jnp.where(rows == 0, prev_row, pltpu.roll(x, 1, axis=0))
    x_next = jnp.where(rows == tm - 1, next_row, pltpu.roll(x, tm - 1, axis=0))
    return cw_ref[0:1, :] * x_prev + cw_ref[1:2, :] * x + cw_ref[2:3, :] * x_next + cb_ref[...]


def _conv3_specs(tm, length, ct, lane_block):
    hb, nhb = tm // HALO, length // HALO
    return [pl.BlockSpec((1, tm, ct), lambda b, r, j: (b, r, lane_block(j))),
            pl.BlockSpec((1, HALO, ct), lambda b, r, j: (b, jnp.maximum(r * hb - 1, 0), lane_block(j))),
            pl.BlockSpec((1, HALO, ct), lambda b, r, j: (b, jnp.minimum((r + 1) * hb, nhb - 1), lane_block(j))),
            pl.BlockSpec((3, ct), lambda b, r, j: (0, lane_block(j))),
            pl.BlockSpec((1, ct), lambda b, r, j: (0, lane_block(j)))]


def _hy_conv_kernel(*refs):
    r, nr = pl.program_id(1), pl.num_programs(1)
    u = [_conv3(*refs[5 * a:5 * a + 5], r, nr) for a in range(3)]
    x0_ref, p_ref = refs[15:]
    x0_ref[0] = u[0].astype(x0_ref.dtype)
    p_ref[0] = (u[2] * u[1]).astype(p_ref.dtype)


def _hy_conv(z, conv_w, conv_b, *, tm):
    bsz, length, _ = z.shape
    nj = D_MODEL // HY_CT
    cw, cb = conv_w.astype(F32), conv_b.astype(F32).reshape(1, -1)
    in_specs, args = [], []
    for a in range(3):
        in_specs += _conv3_specs(tm, length, HY_CT, lambda j, a=a: a * nj + j)
        args += [z, z, z, cw, cb]
    out_spec = pl.BlockSpec((1, tm, HY_CT), lambda b, r, j: (b, r, j))
    return pl.pallas_call(
        _hy_conv_kernel,
        grid=(bsz, length // tm, nj),
        in_specs=in_specs,
        out_specs=[out_spec, out_spec],
        out_shape=[jax.ShapeDtypeStruct((bsz, length, D_MODEL), BF16)] * 2,
        compiler_params=_params("parallel", "parallel", "parallel"),
        name="hy_conv",
    )(*args)


def _dft_cs(rows, cols, modulus):
    m = (rows.astype(jnp.int32)[:, None] * cols.astype(jnp.int32)[None, :]) % modulus
    theta = m.astype(F32) * (2.0 * math.pi / modulus)
    return jnp.cos(theta), jnp.sin(theta)


def _cplx_rows(c, s):
    return jnp.concatenate([jnp.concatenate([c, s], axis=1), jnp.concatenate([-s, c], axis=1)], axis=0)


def _hy_s1_kernel(*refs, real):
    if real:
        zr_ref, f_ref, o_ref = refs
        z = zr_ref[0]
    else:
        zr_ref, zi_ref, f_ref, o_ref = refs
        z = jnp.concatenate([zr_ref[0], zi_ref[0]], axis=0)
    res = jnp.dot(f_ref[...], z, preferred_element_type=F32)
    o_ref[0, 0] = res[:HY_N1].astype(o_ref.dtype)
    o_ref[0, 1] = res[HY_N1:].astype(o_ref.dtype)


def _hy_stage1(p, f, *, real):
    nseq, length, d = p.shape
    n1h = HY_N1 // 2
    cols = (length // n1h) * d
    pv = p.reshape(nseq, n1h, cols)
    npair = nseq if real else nseq // 2
    tc = HY_TC
    in_specs = [pl.BlockSpec((1, n1h, tc), lambda q, j: (q, 0, j))]
    args = [pv]
    if not real:
        in_specs.append(pl.BlockSpec((1, n1h, tc), lambda q, j: (q + npair, 0, j)))
        args.append(pv)
    in_specs.append(pl.BlockSpec(f.shape, lambda q, j: (0, 0)))
    return pl.pallas_call(
        functools.partial(_hy_s1_kernel, real=real),
        grid=(npair, cols // tc),
        in_specs=in_specs,
        out_specs=pl.BlockSpec((1, 2, HY_N1, tc), lambda q, j: (q, 0, 0, j)),
        out_shape=jax.ShapeDtypeStruct((npair, 2, HY_N1, cols), BF16),
        compiler_params=_params("parallel", "parallel"),
        name="hy_stage1",
    )(*args, f)


def _hy_s2f_kernel(a_ref, g_ref, o_ref):
    a = jnp.concatenate([a_ref[0, 0, 0], a_ref[0, 1, 0]], axis=0)
    o_ref[0, 0] = jnp.dot(g_ref[0], a, preferred_element_type=F32)


def _hy_stage2_filter(a, g):
    nseq = a.shape[0]
    av = a.reshape(nseq, 2, HY_N1, HY_N2, D_MODEL)
    return pl.pallas_call(
        _hy_s2f_kernel,
        grid=(HY_N1, nseq),
        in_specs=[pl.BlockSpec((1, 2, 1, HY_N2, D_MODEL), lambda k, q: (q, 0, k, 0, 0)),
                  pl.BlockSpec((1, 2 * HY_N2, 2 * HY_N2), lambda k, q: (k, 0, 0))],
        out_specs=pl.BlockSpec((1, 1, 2 * HY_N2, D_MODEL), lambda k, q: (q, k, 0, 0)),
        out_shape=jax.ShapeDtypeStruct((nseq, HY_N1, 2 * HY_N2, D_MODEL), F32),
        compiler_params=_params("parallel", "parallel"),
        name="hy_stage2_filter",
    )(av, g)


def _cmul(x, h, n):
    xr, xi, hr, hi = x[:n], x[n:], h[:n], h[n:]
    return jnp.concatenate([xr * hr - xi * hi, xr * hi + xi * hr], axis=0)


def _hy_s2_kernel(a_ref, g_ref, gi_ref, h_ref, o_ref):
    a = jnp.concatenate([a_ref[0, 0, 0], a_ref[0, 1, 0]], axis=0)
    x = jnp.dot(g_ref[0], a, preferred_element_type=F32)
    sign = (1 - 2 * (pl.program_id(0) % 2)).astype(F32)
    y = _cmul(x, h_ref[0, 0] + sign * h_ref[1, 0], HY_N2).astype(BF16)
    b = jnp.dot(gi_ref[0], y, preferred_element_type=F32)
    o_ref[0, 0, 0] = b[:HY_N2].astype(o_ref.dtype)
    o_ref[0, 1, 0] = b[HY_N2:].astype(o_ref.dtype)


def _hy_stage2(a, g, gi, hspec):
    npair = a.shape[0]
    av = a.reshape(npair, 2, HY_N1, HY_N2, D_MODEL)
    blk = pl.BlockSpec((1, 2, 1, HY_N2, D_MODEL), lambda k, q: (q, 0, k, 0, 0))
    mat = pl.BlockSpec((1, 2 * HY_N2, 2 * HY_N2), lambda k, q: (k, 0, 0))
    out = pl.pallas_call(
        _hy_s2_kernel,
        grid=(HY_N1, npair),
        in_specs=[blk, mat, mat,
                  pl.BlockSpec((2, 1, 2 * HY_N2, D_MODEL), lambda k, q: (0, k, 0, 0))],
        out_specs=blk,
        out_shape=jax.ShapeDtypeStruct(av.shape, BF16),
        compiler_params=_params("parallel", "parallel"),
        name="hy_stage2",
    )(av, g, gi, hspec)
    return out.reshape(a.shape)


def _hy_s1inv_kernel(b_ref, f_ref, x0_ref, p_ref, bias_ref, o_ref):
    bc = jnp.concatenate([b_ref[0, 0], b_ref[0, 1]], axis=0)
    y = jnp.dot(f_ref[...], bc, preferred_element_type=F32)
    n1h = HY_N1 // 2
    for half in range(2):
        conv = y[half * n1h:(half + 1) * n1h]
        u = p_ref[half, 0].astype(F32)
        o_ref[half, 0] = (x0_ref[half, 0].astype(F32) * (conv + u * bias_ref[...])).astype(o_ref.dtype)


def _hy_stage1_inv(b, f, x0, p, bias):
    npair, _, _, cols = b.shape
    bsz, length, d = x0.shape
    n1h = HY_N1 // 2
    tc = HY_TC
    view = lambda t: t.reshape(2, npair, n1h, cols)
    seq = pl.BlockSpec((2, 1, n1h, tc), lambda q, j: (0, q, 0, j))
    out = pl.pallas_call(
        _hy_s1inv_kernel,
        grid=(npair, cols // tc),
        in_specs=[pl.BlockSpec((1, 2, HY_N1, tc), lambda q, j: (q, 0, 0, j)),
                  pl.BlockSpec(f.shape, lambda q, j: (0, 0)),
                  seq, seq,
                  pl.BlockSpec((1, tc), lambda q, j: (0, 0))],
        out_specs=seq,
        out_shape=jax.ShapeDtypeStruct((2, npair, n1h, cols), BF16),
        compiler_params=_params("parallel", "parallel"),
        name="hy_stage1_inv",
    )(b, f, view(x0), view(p), jnp.tile(bias.astype(F32).reshape(1, d), (1, tc // d)))
    return out.reshape(bsz, length, d)


def _hy_ctx_filter_kernel(ha_ref, hb_ref, f_ref, o_ref):
    h2 = jnp.concatenate([ha_ref[...], hb_ref[...]], axis=0)
    o_ref[...] = jnp.dot(f_ref[...], h2, preferred_element_type=F32)


def _hy_ctx_kernel(zr_ref, zi_ref, f_ref, fi_ref, h_ref, x0_ref, p_ref, bias_ref, o_ref):
    length = zr_ref.shape[1]
    z = jnp.concatenate([zr_ref[0], zi_ref[0]], axis=0)
    x = jnp.dot(f_ref[...], z, preferred_element_type=F32)
    y = _cmul(x, h_ref[...], 2 * length).astype(BF16)
    out = jnp.dot(fi_ref[...], y, preferred_element_type=F32)
    for half in range(2):
        conv = out[half * length:(half + 1) * length]
        u = p_ref[half, 0].astype(F32)
        o_ref[half, 0] = (x0_ref[half, 0].astype(F32) * (conv + u * bias_ref[...])).astype(o_ref.dtype)


def _hy_ctx_conv(x0, p, ha, hb, bias):
    bsz, length, d = p.shape
    npair = bsz // 2
    n = 2 * length
    k = jnp.arange(n)
    c, s = _dft_cs(k, jnp.arange(length), n)
    f = _cplx_rows(c, s).astype(BF16)
    fi = (_cplx_rows(c.T, -s.T) * (1.0 / n)).astype(BF16)
    cf, sf = _dft_cs(k, k, n)
    ffilt = jnp.concatenate([cf, -sf], axis=0).astype(BF16)
    full = lambda t: pl.BlockSpec(t.shape, lambda *_: (0,) * t.ndim)
    hspec = pl.pallas_call(
        _hy_ctx_filter_kernel,
        out_shape=jax.ShapeDtypeStruct((2 * n, d), F32),
        compiler_params=_params(),
        name="hy_ctx_filter",
    )(ha, hb, ffilt)
    view = lambda t: t.reshape(2, npair, length, d)
    seq = pl.BlockSpec((2, 1, length, d), lambda q: (0, q, 0, 0))
    bias2 = bias.astype(F32).reshape(1, d)
    out = pl.pallas_call(
        _hy_ctx_kernel,
        grid=(npair,),
        in_specs=[pl.BlockSpec((1, length, d), lambda q: (q, 0, 0)),
                  pl.BlockSpec((1, length, d), lambda q: (q + npair, 0, 0)),
                  full(f), full(fi), full(hspec), seq, seq, full(bias2)],
        out_specs=seq,
        out_shape=jax.ShapeDtypeStruct((2, npair, length, d), BF16),
        compiler_params=_params("parallel"),
        name="hy_ctx_conv",
    )(p, p, f, fi, hspec, view(x0), view(p), bias2)
    return out.reshape(bsz, length, d)


def _hy_lat_conv(x0, p, ha, hb, bias):
    bsz, length, d = p.shape
    assert 2 * length == HY_N1 * HY_N2 and d == D_MODEL and bsz % 2 == 0
    n = 2 * length
    k1, n1 = jnp.arange(HY_N1), jnp.arange(HY_N1 // 2)
    c1, s1 = _dft_cs(k1, n1, HY_N1)
    f1 = _cplx_rows(c1, s1).astype(BF16)
    f1_real = jnp.concatenate([c1, -s1], axis=0).astype(BF16)
    f1_inv = (_cplx_rows(c1.T, -s1.T) * (1.0 / n)).astype(BF16)
    k = (k1[:, None] + HY_N1 * jnp.arange(HY_N2)[None, :]).reshape(-1)
    c2, s2 = _dft_cs(k, jnp.arange(HY_N2), n)
    c2, s2 = c2.reshape(HY_N1, HY_N2, HY_N2), s2.reshape(HY_N1, HY_N2, HY_N2)
    g = jax.vmap(_cplx_rows)(c2, s2).astype(BF16)
    gi = jax.vmap(_cplx_rows)(c2.transpose(0, 2, 1), -s2.transpose(0, 2, 1)).astype(BF16)
    hfilt = jnp.stack([ha, hb], axis=0)
    hspec = _hy_stage2_filter(_hy_stage1(hfilt, f1_real, real=True), g)
    a = _hy_stage1(p, f1, real=False)
    b = _hy_stage2(a, g, gi, hspec)
    return _hy_stage1_inv(b, f1_inv, x0, p, bias)


def _hy_mixer(z_lat, z_ctx, conv_w, conv_b, f_w1, f_b1, f_w2, f_b2, f_w3, f_freq, f_bias):
    def run(z, tm, conv_fn):
        length = z.shape[1]
        x0, p = _hy_conv(z, conv_w, conv_b, tm=tm)
        ha, hb = _hy_filter(length, f_w1, f_b1, f_w2, f_b2, f_w3, f_freq)
        return conv_fn(x0, p, ha, hb, f_bias)

    return run(z_lat, 512, _hy_lat_conv), run(z_ctx, z_ctx.shape[1], _hy_ctx_conv)


def _ml_qkv_kernel(xm_ref, xp_ref, xn_ref, cw_ref, cb_ref, wq_ref, wk_ref, wv_ref, wg_ref, bg_ref,
                   q_ref, k_ref, v_ref, xc_ref, g_ref):
    j = pl.program_id(2)
    xm = xm_ref[0]
    conv = _conv3(xm_ref, xp_ref, xn_ref, cw_ref, cb_ref, pl.program_id(1), pl.num_programs(1))
    xc = _silu(conv).astype(BF16)
    q = jnp.dot(xc, wq_ref[0], preferred_element_type=F32).astype(BF16)
    k = jnp.dot(xc, wk_ref[0], preferred_element_type=F32).astype(BF16)
    v = jnp.dot(xm, wv_ref[0], preferred_element_type=F32).astype(BF16)
    q_ref[0], k_ref[0], v_ref[0], xc_ref[0] = q, k, v, xc
    contrib = (jnp.dot(q, wg_ref[0], preferred_element_type=F32)
               + jnp.dot(k, wg_ref[1], preferred_element_type=F32)
               + jnp.dot(v, wg_ref[2], preferred_element_type=F32))

    @pl.when(j == 0)
    def _():
        g_ref[0] = contrib

    @pl.when(j > 0)
    def _():
        g_ref[0] += contrib

    @pl.when(j == pl.num_programs(2) - 1)
    def _():
        g = g_ref[0] + bg_ref[...]
        lane = lax.broadcasted_iota(jnp.int32, g.shape, 1)
        g_ref[0] = jnp.where(lane % (2 * ML_HEADS) >= ML_HEADS, jax.nn.log_sigmoid(g), g)


def _ml_qkv(z, conv_w, conv_b, wq, wk, wv, wg, bg, *, tm):
    bsz, length, _ = z.shape
    nj = ML_INNER // ML_CT
    seq = lambda: pl.BlockSpec((1, tm, ML_CT), lambda b, r, j: (b, r, j))
    shp = jax.ShapeDtypeStruct
    return pl.pallas_call(
        _ml_qkv_kernel,
        grid=(bsz, length // tm, nj),
        in_specs=_conv3_specs(tm, length, ML_CT, lambda j: j) + [
                  pl.BlockSpec((1, ML_CT, ML_CT), lambda b, r, j: (j, 0, 0)),
                  pl.BlockSpec((1, ML_CT, ML_CT), lambda b, r, j: (j, 0, 0)),
                  pl.BlockSpec((1, ML_CT, ML_CT), lambda b, r, j: (j, 0, 0)),
                  pl.BlockSpec((3, ML_CT, LANES), lambda b, r, j: (0, j, 0)),
                  pl.BlockSpec((1, LANES), lambda b, r, j: (0, 0))],
        out_specs=[seq(), seq(), seq(), seq(),
                   pl.BlockSpec((1, tm, LANES), lambda b, r, j: (b, r, 0))],
        out_shape=[shp((bsz, length, ML_INNER), BF16)] * 4 + [shp((bsz, length, LANES), F32)],
        compiler_params=_params("parallel", "parallel", "arbitrary"),
        name="ml_qkv",
    )(z, z, z, conv_w, conv_b, wq, wk, wv, wg, bg)


def _ml_scan_kernel(q_ref, k_ref, v_ref, g_ref, c0_ref, n0_ref, m0_ref, h_ref, c_ref, n_ref, m_ref,
                    *, reverse, z):
    @pl.when(pl.program_id(1) == 0)
    def _():
        c_ref[...] = c0_ref[...]
        n_ref[...] = n0_ref[...]
        m_ref[...] = m0_ref[...]

    ch = q_ref.shape[1]
    mask = _tri_mask(ch, reverse)
    gates = g_ref[0]
    bcum = _cumsum_f32(mask.astype(BF16), gates)
    gates_t, bcum_t = gates.T, bcum.T
    end = 0 if reverse else ch - 1
    for h in range(ML_HEADS):
        li, lf = z * 2 * ML_HEADS + h, z * 2 * ML_HEADS + ML_HEADS + h
        sl = slice(h * ML_DH, (h + 1) * ML_DH)
        q, v = q_ref[0, :, sl], v_ref[0, :, sl]
        k = k_ref[0, :, sl].astype(F32) * ML_DH ** -0.5
        b_col, b_row = bcum[:, lf:lf + 1], bcum_t[lf:lf + 1, :]
        i_col, i_row = gates[:, li:li + 1], gates_t[li:li + 1, :]
        b_end = b_col[end:end + 1, :]
        m_prev = m_ref[0, h][0:1, 0:1]
        d_intra = jnp.where(mask, b_col - b_row + i_row, -jnp.inf)
        d_inter = b_col + m_prev
        m_tok = jnp.maximum(jnp.max(d_intra, axis=1, keepdims=True), d_inter)
        w_inter = jnp.exp(d_inter - m_tok)
        s = lax.dot_general(q, k.astype(BF16), _NT_DIMS, preferred_element_type=F32) * jnp.exp(d_intra - m_tok)
        c_mat = c_ref[0, h]
        n_vec = n_ref[0, h]
        num = (jnp.dot(s.astype(BF16), v, preferred_element_type=F32)
               + w_inter * jnp.dot(q, c_mat.astype(BF16), preferred_element_type=F32))
        den = (jnp.sum(s, axis=1, keepdims=True)
               + w_inter * jnp.sum(q.astype(F32) * n_vec, axis=1, keepdims=True))
        h_ref[0, :, sl] = (num / jnp.maximum(jnp.abs(den), jnp.exp(-m_tok))).astype(h_ref.dtype)
        d_state = b_end - b_col + i_col
        m_new = jnp.maximum(b_end + m_prev, jnp.max(d_state, axis=0, keepdims=True))
        w_prev = jnp.exp(b_end + m_prev - m_new)
        kw = k * jnp.exp(d_state - m_new)
        c_ref[0, h] = w_prev * c_mat + lax.dot_general(kw.astype(BF16), v, _TN_DIMS, preferred_element_type=F32)
        n_ref[0, h] = w_prev * n_vec + jnp.sum(kw, axis=0, keepdims=True)
        m_ref[0, h] = jnp.broadcast_to(m_new, m_ref.shape[2:])


def _ml_scan(q, k, v, gates, state, *, reverse, z, ch):
    bsz, length, _ = q.shape
    nc = length // ch
    cix = (lambda i: nc - 1 - i) if reverse else (lambda i: i)
    seq = lambda w: pl.BlockSpec((1, ch, w), lambda b, i: (b, cix(i), 0))
    st_specs = [pl.BlockSpec((1,) + s.shape[1:], lambda b, i: (b, 0, 0, 0)) for s in state]
    shp = jax.ShapeDtypeStruct
    out = pl.pallas_call(
        functools.partial(_ml_scan_kernel, reverse=reverse, z=z),
        grid=(bsz, nc),
        in_specs=[seq(ML_INNER), seq(ML_INNER), seq(ML_INNER), seq(LANES)] + st_specs,
        out_specs=[seq(ML_INNER)] + st_specs,
        out_shape=[shp((bsz, length, ML_INNER), BF16)] + [shp(s.shape, F32) for s in state],
        compiler_params=_params("parallel", "arbitrary"),
        name="ml_scan",
    )(q, k, v, gates, *state)
    return out[0], tuple(out[1:])


def _ml_out_kernel(hf_ref, hb_ref, xc_ref, z_ref, ng_ref, sk_ref, o_ref):
    for h in range(ML_HEADS):
        sl = slice(h * ML_DH, (h + 1) * ML_DH)
        o = hf_ref[0, :, sl].astype(F32) + hb_ref[0, :, sl].astype(F32)
        o = o - jnp.mean(o, axis=-1, keepdims=True)
        o = o * lax.rsqrt(jnp.mean(o * o, axis=-1, keepdims=True) + EPS) * ng_ref[:, sl]
        t = (o + sk_ref[:, sl] * xc_ref[0, :, sl].astype(F32)) * _silu(z_ref[0, :, sl].astype(F32))
        o_ref[0, :, sl] = t.astype(o_ref.dtype)


def _ml_out(h_f, h_b, xc, z, norm_g, skip, *, tm):
    bsz, length, _ = h_f.shape
    seq = pl.BlockSpec((1, tm, ML_INNER), lambda b, i: (b, i, 0))
    vec = pl.BlockSpec((1, ML_INNER), lambda b, i: (0, 0))
    return pl.pallas_call(
        _ml_out_kernel,
        grid=(bsz, length // tm),
        in_specs=[seq, seq, seq, pl.BlockSpec((1, tm, ML_INNER), lambda b, i: (b, i, 1)), vec, vec],
        out_specs=seq,
        out_shape=jax.ShapeDtypeStruct((bsz, length, ML_INNER), BF16),
        compiler_params=_params("parallel", "parallel"),
        name="ml_out",
    )(h_f, h_b, xc, z, norm_g.astype(F32).reshape(1, ML_INNER), skip.astype(F32).reshape(1, ML_INNER))


def _ml_weights(w_q, w_k, w_v, w_gates, b_gates):
    nb = ML_CT // ML_BLOCK
    eye = jnp.eye(nb, dtype=F32)

    def dense(w):
        w = w.astype(F32).reshape(ML_INNER // ML_CT, nb, ML_BLOCK, ML_BLOCK)
        return jnp.einsum('tncd,nm->tncmd', w, eye).reshape(ML_INNER // ML_CT, ML_CT, ML_CT).astype(BF16)

    ng = 2 * 2 * ML_HEADS
    wg = w_gates.astype(F32).reshape(2, 3, ML_INNER, 2 * ML_HEADS).transpose(1, 2, 0, 3).reshape(3, ML_INNER, ng)
    wg = jnp.pad(wg, ((0, 0), (0, 0), (0, LANES - ng))).astype(BF16)
    bg = jnp.pad(b_gates.astype(F32).reshape(1, ng), ((0, 0), (0, LANES - ng)))
    return dense(w_q), dense(w_k), dense(w_v), wg, bg


def _ml_mixer(z_lat, z_ctx, conv_w, conv_b, w_q, w_k, w_v, w_gates, b_gates, norm_g, skip):
    wq, wk, wv, wg, bg = _ml_weights(w_q, w_k, w_v, w_gates, b_gates)
    cw, cb = conv_w.astype(F32), conv_b.astype(F32).reshape(1, ML_INNER)
    bsz = z_lat.shape[0]
    state = (jnp.zeros((bsz, ML_HEADS, ML_DH, ML_DH), F32),
             jnp.zeros((bsz, ML_HEADS, 1, ML_DH), F32),
             jnp.full((bsz, ML_HEADS, 8, LANES), -jnp.inf, F32))

    def bidirectional(z, st_f, st_b, tm):
        q, k, v, xc, gates = _ml_qkv(z, cw, cb, wq, wk, wv, wg, bg, tm=tm)
        h_f, st_f = _ml_scan(q, k, v, gates, st_f, reverse=False, z=0, ch=ML_CHUNK)
        h_b, st_b = _ml_scan(q, k, v, gates, st_b, reverse=True, z=1, ch=ML_CHUNK)
        return _ml_out(h_f, h_b, xc, z, norm_g, skip, tm=tm), st_f, st_b

    t_ctx, st_f, st_b = bidirectional(z_ctx, state, state, z_ctx.shape[1])
    t_lat, _, _ = bidirectional(z_lat, st_f, st_b, 512)
    return t_lat, t_ctx


def kernel(x, c, ctx, c_ctx, mod_w, mod_b, norm_g, ffn_w_in, ffn_w_out, gla_w_in, gla_w_gate, gla_b_gate, gla_norm_g, gla_w_out, hy_w_in, hy_conv_w, hy_conv_b, hy_f_w1, hy_f_b1, hy_f_w2, hy_f_b2, hy_f_w3, hy_f_freq, hy_f_bias, hy_w_out, ml_w_in, ml_conv_w, ml_conv_b, ml_w_q, ml_w_k, ml_w_v, ml_w_gates, ml_b_gates, ml_norm_g, ml_skip, ml_w_out, s5_lam_re, s5_lam_im, s5_log_dt, s5_b_re, s5_b_im, s5_c_re, s5_c_im, s5_d, s5_w_glu):
    bsz, seq, d = x.shape
    ctx_len = ctx.shape[1]
    depth = mod_w.shape[0]
    mods = _modulation(c, c_ctx, mod_w, mod_b)
    lat = x.astype(F32)
    cx = ctx.astype(F32)
    tm_lat, tm_ctx = 512, ctx_len
    for i in range(depth):
        kind, j = i % N_MIXERS, i // N_MIXERS
        last = i == depth - 1
        colmajor = kind >= 2
        mod_l = mods[i, :bsz]
        mod_c = mods[i, bsz:bsz + 1]
        win, wout = _ffn_weights(ffn_w_in[i], ffn_w_out[i])
        wmix = None
        tmajor = False
        if kind == 3:
            tmajor = True
            u_lat = _norm_mod(lat, mod_l, norm_g[i, 0], colmajor=True, tm=tm_lat, tmajor=True)
            u_ctx = _norm_mod(cx, mod_c, norm_g[i, 0], colmajor=False, tm=tm_ctx, tmajor=True)
            y_lat, y_ctx = _s5_mixer(u_lat.reshape(1, seq * bsz, d), u_ctx.reshape(1, ctx_len * bsz, d), bsz,
                                     s5_lam_re[j], s5_lam_im[j], s5_log_dt[j], s5_b_re[j], s5_b_im[j],
                                     s5_c_re[j], s5_c_im[j], s5_d[j], s5_w_glu[j])
            y_lat = y_lat.reshape(1, seq, bsz * d)
            y_ctx = y_ctx.reshape(1, ctx_len, bsz * d)
        elif kind == 0:
            w_main, w_glr, wg, bg = _gla_weights(gla_w_in[j], gla_w_gate[j], gla_b_gate[j])
            z_lat, glr_lat = _norm_proj(lat, mod_l, norm_g[i, 0], [w_main, w_glr], [BF16, F32],
                                        colmajor=False, tm=tm_lat)
            z_ctx, glr_ctx = _norm_proj(cx, mod_c, norm_g[i, 0], [w_main, w_glr], [BF16, F32],
                                        colmajor=False, tm=tm_ctx)
            y_lat, y_ctx = _gla_mixer(z_lat, glr_lat, z_ctx, glr_ctx, wg, bg, gla_norm_g[j])
            wmix = gla_w_out[j].astype(BF16)
        elif kind == 2:
            w_in = ml_w_in[j].astype(BF16)
            (z_lat,) = _norm_proj(lat, mod_l, norm_g[i, 0], [w_in], [BF16], colmajor=True, tm=tm_lat)
            (z_ctx,) = _norm_proj(cx, mod_c, norm_g[i, 0], [w_in], [BF16], colmajor=False, tm=tm_ctx)
            y_lat, y_ctx = _ml_mixer(z_lat, z_ctx, ml_conv_w[j], ml_conv_b[j], ml_w_q[j], ml_w_k[j], ml_w_v[j],
                                     ml_w_gates[j], ml_b_gates[j], ml_norm_g[j], ml_skip[j])
            wmix = ml_w_out[j].astype(BF16)
        else:
            w_in = hy_w_in[j].astype(BF16)
            (z_lat,) = _norm_proj(lat, mod_l, norm_g[i, 0], [w_in], [BF16], colmajor=False, tm=tm_lat)
            (z_ctx,) = _norm_proj(cx, mod_c, norm_g[i, 0], [w_in], [BF16], colmajor=False, tm=tm_ctx)
            y_lat, y_ctx = _hy_mixer(z_lat, z_ctx, hy_conv_w[j], hy_conv_b[j], hy_f_w1[j], hy_f_b1[j],
                                     hy_f_w2[j], hy_f_b2[j], hy_f_w3[j], hy_f_freq[j], hy_f_bias[j])
            wmix = hy_w_out[j].astype(BF16)
        lat = _post_ffn(lat, y_lat, mod_l, norm_g[i], wmix, win, wout, colmajor=colmajor, tm=tm_lat,
                        tmajor=tmajor)
        if not last:
            cx = _post_ffn(cx, y_ctx, mod_c, norm_g[i], wmix, win, wout, colmajor=False, tm=tm_ctx,
                           tmajor=tmajor)
    return lat
```

```python
import functools
import math

import jax
import jax.numpy as jnp
from jax import lax
from jax.experimental import pallas as pl
from jax.experimental.pallas import tpu as pltpu

F32 = jnp.float32
BF16 = jnp.bfloat16

D_MODEL = 1024
GRID_W = 64
N_MOD = 6
EPS = 1e-6
FFN_HIDDEN = 2816
FFN_CHUNK = 256
N_FFN_CHUNKS = FFN_HIDDEN // FFN_CHUNK
MOD_ROWS = 16
VMEM_LIMIT = 56 * 1024 * 1024
LANES = 128
N_MIXERS = 4
PROJ_TN = 512

GLA_HEADS = 4
GLA_QK = D_MODEL // 2
GLA_VD = D_MODEL
GLA_DK = GLA_QK // GLA_HEADS
GLA_DV = GLA_VD // GLA_HEADS
GLA_RANK = 16
GLA_GATE_NORM = 16.0
GLA_CHUNK = 64
GLA_ROWS = 4

HALO = 16
HY_BANDS = 16
HY_DECAY_TARGET = 1e-2
HY_FAST_DECAY = 0.3
HY_SLOW_DECAY = 1.5
HY_CT = 256
HY_N1, HY_N2 = 64, 128
HY_TC = 8192

ML_HEADS = 4
ML_INNER = 2 * D_MODEL
ML_DH = ML_INNER // ML_HEADS
ML_BLOCK = 4
ML_CHUNK = 256
ML_CT = 256

S5_GROUP = 16
S5_GROUPS = D_MODEL // S5_GROUP
S5_STATE = 64
S5_TILE_G = LANES // S5_GROUP
S5_NT = S5_GROUPS // S5_TILE_G
S5_SL = S5_TILE_G * S5_STATE
S5_TB = 64
S5_KT = 4


def _rms(x, g):
    return x * lax.rsqrt(jnp.mean(x * x, axis=-1, keepdims=True) + EPS) * g


def _silu(x):
    return x * jax.nn.sigmoid(x)


def _params(*sem):
    return pltpu.CompilerParams(dimension_semantics=sem, vmem_limit_bytes=VMEM_LIMIT)


def _mod_kernel(cc_ref, w_ref, b_ref, o_ref):
    a = _silu(cc_ref[...])
    o_ref[0] = jnp.dot(a.astype(BF16), w_ref[0].astype(BF16), preferred_element_type=F32) + b_ref[0]


def _modulation(c, c_ctx, mod_w, mod_b):
    depth, d, n = mod_w.shape
    bsz = c.shape[0]
    cc = jnp.concatenate([c.astype(F32), c_ctx.astype(F32)[None],
                          jnp.zeros((MOD_ROWS - bsz - 1, d), F32)], axis=0)
    tn = 1536
    out = pl.pallas_call(
        _mod_kernel,
        grid=(depth, n // tn),
        in_specs=[pl.BlockSpec((MOD_ROWS, d), lambda i, j: (0, 0)),
                  pl.BlockSpec((1, d, tn), lambda i, j: (i, 0, j)),
                  pl.BlockSpec((1, 1, tn), lambda i, j: (i, 0, j))],
        out_specs=pl.BlockSpec((1, MOD_ROWS, tn), lambda i, j: (i, 0, j)),
        out_shape=jax.ShapeDtypeStruct((depth, MOD_ROWS, n), F32),
        compiler_params=_params("parallel", "parallel"),
        name="adaln_mod",
    )(cc, mod_w, mod_b.reshape(depth, 1, n))
    return out.reshape(depth, MOD_ROWS, N_MOD, d)


def _to_col_major(x):
    bsz, length, d = x.shape
    return x.reshape(bsz, length // GRID_W, GRID_W, d).transpose(0, 2, 1, 3).reshape(bsz, length, d)


def _from_col_major(x):
    bsz, length, d = x.shape
    return x.reshape(bsz, GRID_W, length // GRID_W, d).transpose(0, 2, 1, 3).reshape(bsz, length, d)


def _seq_spec(tmajor, tm, width):
    if tmajor:
        return pl.BlockSpec((1, tm, width), lambda b, i: (0, i, b))
    return pl.BlockSpec((1, tm, width), lambda b, i: (b, i, 0))


def _mod_spec(mod):
    if mod.shape[0] > 1:
        return pl.BlockSpec((1, N_MOD, D_MODEL), lambda b, i: (b, 0, 0))
    return pl.BlockSpec((1, N_MOD, D_MODEL), lambda b, i: (0, 0, 0))


def _norm_kernel(x_ref, m_ref, g_ref, o_ref):
    h = _rms(x_ref[0], g_ref[...]) * (1.0 + m_ref[0, 1:2, :]) + m_ref[0, 0:1, :]
    o_ref[0] = h.astype(o_ref.dtype)


def _norm_mod(x, mod, g, *, tm, tmajor=False, out_dtype=F32):
    bsz, length, d = x.shape
    out_shape = (1, length, bsz * d) if tmajor else (bsz, length, d)
    return pl.pallas_call(
        _norm_kernel,
        grid=(bsz, length // tm),
        in_specs=[_seq_spec(False, tm, d), _mod_spec(mod), pl.BlockSpec((1, d), lambda b, i: (0, 0))],
        out_specs=_seq_spec(tmajor, tm, d),
        out_shape=jax.ShapeDtypeStruct(out_shape, out_dtype),
        compiler_params=_params("parallel", "parallel"),
        name="norm_mod",
    )(x, mod, g.reshape(1, d))


def _proj_kernel(*refs, n_w):
    x_ref, m_ref, g_ref = refs[:3]
    w_refs, o_refs = refs[3:3 + n_w], refs[3 + n_w:]
    h = (_rms(x_ref[0], g_ref[...]) * (1.0 + m_ref[0, 1:2, :]) + m_ref[0, 0:1, :]).astype(BF16)
    for w_ref, o_ref in zip(w_refs, o_refs):
        n = w_ref.shape[1]
        tn = min(n, PROJ_TN)
        for n0 in range(0, n, tn):
            o_ref[0, :, n0:n0 + tn] = jnp.dot(h, w_ref[:, n0:n0 + tn],
                                              preferred_element_type=F32).astype(o_ref.dtype)


def _norm_proj(x, mod, g, ws, dtypes, *, tm):
    bsz, length, d = x.shape
    full = lambda shape: pl.BlockSpec(shape, lambda b, i: (0,) * len(shape))
    return pl.pallas_call(
        functools.partial(_proj_kernel, n_w=len(ws)),
        grid=(bsz, length // tm),
        in_specs=[_seq_spec(False, tm, d), _mod_spec(mod), full((1, d))] + [full(w.shape) for w in ws],
        out_specs=[_seq_spec(False, tm, w.shape[1]) for w in ws],
        out_shape=[jax.ShapeDtypeStruct((bsz, length, w.shape[1]), dt) for w, dt in zip(ws, dtypes)],
        compiler_params=_params("parallel", "parallel"),
        name="norm_proj",
    )(x, mod, g.reshape(1, d), *ws)


def _ffn_kernel(*refs, has_wmix):
    if has_wmix:
        x_ref, y_ref, m_ref, g_ref, wmix_ref, win_ref, wout_ref, o_ref = refs
        y = jnp.dot(y_ref[0], wmix_ref[...], preferred_element_type=F32)
    else:
        x_ref, y_ref, m_ref, g_ref, win_ref, wout_ref, o_ref = refs
        y = y_ref[0].astype(F32)
    lat = x_ref[0] + m_ref[0, 2:3, :] * _rms(y, g_ref[1:2, :])
    h = (_rms(lat, g_ref[2:3, :]) * (1.0 + m_ref[0, 4:5, :]) + m_ref[0, 3:4, :]).astype(BF16)

    def body(c, acc):
        gate = jnp.dot(h, win_ref[0, c], preferred_element_type=F32)
        up = jnp.dot(h, win_ref[1, c], preferred_element_type=F32)
        a = (_silu(gate) * up).astype(BF16)
        return acc + jnp.dot(a, wout_ref[c], preferred_element_type=F32)

    acc = lax.fori_loop(0, N_FFN_CHUNKS, body, jnp.zeros(lat.shape, F32), unroll=True)
    o_ref[0] = lat + m_ref[0, 5:6, :] * _rms(acc, g_ref[3:4, :])


def _post_ffn(x, y, mod, g4, wmix, win, wout, *, tm, tmajor=False):
    bsz, length, d = x.shape
    width = y.shape[-1] // bsz if tmajor else y.shape[-1]
    full = lambda shape: pl.BlockSpec(shape, lambda b, i: (0,) * len(shape))
    in_specs = [_seq_spec(False, tm, d), _seq_spec(tmajor, tm, width), _mod_spec(mod), full((4, d))]
    args = [x, y, mod, g4]
    if wmix is not None:
        in_specs.append(full(wmix.shape))
        args.append(wmix)
    in_specs += [full(win.shape), full(wout.shape)]
    args += [win, wout]
    return pl.pallas_call(
        functools.partial(_ffn_kernel, has_wmix=wmix is not None),
        grid=(bsz, length // tm),
        in_specs=in_specs,
        out_specs=_seq_spec(False, tm, d),
        out_shape=jax.ShapeDtypeStruct(x.shape, F32),
        compiler_params=_params("parallel", "parallel"),
        name="post_ffn",
    )(*args)


def _ffn_weights(w_in, w_out):
    d = w_in.shape[0]
    win = w_in.astype(BF16).reshape(d, 2, N_FFN_CHUNKS, FFN_CHUNK).transpose(1, 2, 0, 3)
    wout = w_out.astype(BF16).reshape(N_FFN_CHUNKS, FFN_CHUNK, d)
    return win, wout


def _s5_disc_kernel(lre_ref, lim_ref, ldt_ref, bre_ref, bim_ref, lbr_ref, lbi_ref, bbr_ref, bbi_ref):
    lre, lim = lre_ref[...], lim_ref[...]
    dt = jnp.exp(ldt_ref[...])
    mag = jnp.exp(lre * dt)
    lbr = mag * jnp.cos(lim * dt)
    lbi = mag * jnp.sin(lim * dt)
    lbr_ref[...] = lbr
    lbi_ref[...] = lbi
    nr = lbr - 1.0
    den = lre * lre + lim * lim
    cr = (nr * lre + lbi * lim) / den
    ci = (lbi * lre - nr * lim) / den
    for dr in range(2):
        bbr_ref[dr] = cr[dr:dr + 1] * bre_ref[...] - ci[dr:dr + 1] * bim_ref[...]
        bbi_ref[dr] = cr[dr:dr + 1] * bim_ref[...] + ci[dr:dr + 1] * bre_ref[...]


def _s5_discretise(lam_re, lam_im, log_dt, b_re, b_im):
    gp = S5_GROUPS * S5_STATE
    flat = lambda t: t.astype(F32).reshape(2, gp)
    ldt = jnp.repeat(log_dt.astype(F32), S5_STATE, axis=1)
    bt = lambda t: t.astype(F32).reshape(gp, S5_GROUP).T
    shp = jax.ShapeDtypeStruct
    return pl.pallas_call(
        _s5_disc_kernel,
        out_shape=(shp((2, gp), F32), shp((2, gp), F32),
                   shp((2, S5_GROUP, gp), F32), shp((2, S5_GROUP, gp), F32)),
        name="s5_discretise",
    )(flat(lam_re), flat(lam_im), ldt, bt(b_re), bt(b_im))


def _s5_scan_kernel(u_ref, wb_ref, lam_ref, wc_ref, x0_ref, y_ref, xt_ref, bu_scr, st_scr, *, reverse, nb):
    i = pl.program_id(1)

    @pl.when(i == 0)
    def _():
        st_scr[...] = x0_ref[...]

    for kt in range(S5_KT):
        u = u_ref[0, :, kt * LANES:(kt + 1) * LANES].astype(BF16)
        bu_scr[kt] = jnp.dot(u, wb_ref[kt], preferred_element_type=F32)
    for kt in range(S5_KT):
        lr = jnp.broadcast_to(lam_ref[kt, 0], (nb, S5_SL))
        li = jnp.broadcast_to(lam_ref[kt, 1], (nb, S5_SL))
        xr, xi = st_scr[kt, 0], st_scr[kt, 1]
        for s in range(S5_TB):
            r0 = ((S5_TB - 1 - s) if reverse else s) * nb
            xr, xi = (lr * xr - li * xi + bu_scr[kt, r0:r0 + nb, 0:S5_SL],
                      lr * xi + li * xr + bu_scr[kt, r0:r0 + nb, S5_SL:2 * S5_SL])
            bu_scr[kt, r0:r0 + nb, 0:S5_SL] = xr
            bu_scr[kt, r0:r0 + nb, S5_SL:2 * S5_SL] = xi
        st_scr[kt, 0] = xr
        st_scr[kt, 1] = xi
        y_ref[0, :, kt * LANES:(kt + 1) * LANES] = jnp.dot(bu_scr[kt].astype(BF16), wc_ref[kt],
                                                            preferred_element_type=F32)

    @pl.when(i == pl.num_programs(1) - 1)
    def _():
        xt_ref[...] = st_scr[...]


def _s5_scan(u, wb, lam, wc, x0, *, reverse, nb):
    _, rows, d = u.shape
    nt = rows // (S5_TB * nb)
    tix = (lambda i: nt - 1 - i) if reverse else (lambda i: i)
    blk = S5_TB * nb
    shp = jax.ShapeDtypeStruct
    return pl.pallas_call(
        functools.partial(_s5_scan_kernel, reverse=reverse, nb=nb),
        grid=(S5_NT // S5_KT, nt),
        in_specs=[pl.BlockSpec((1, blk, S5_KT * LANES), lambda k, i: (0, tix(i), k)),
                  pl.BlockSpec((S5_KT, LANES, 2 * S5_SL), lambda k, i: (k, 0, 0)),
                  pl.BlockSpec((S5_KT, 2, 1, S5_SL), lambda k, i: (k, 0, 0, 0)),
                  pl.BlockSpec((S5_KT, 2 * S5_SL, LANES), lambda k, i: (k, 0, 0)),
                  pl.BlockSpec((S5_KT, 2, nb, S5_SL), lambda k, i: (k, 0, 0, 0))],
        out_specs=[pl.BlockSpec((1, blk, S5_KT * LANES), lambda k, i: (0, tix(i), k)),
                   pl.BlockSpec((S5_KT, 2, nb, S5_SL), lambda k, i: (k, 0, 0, 0))],
        out_shape=(shp((1, rows, d), F32), shp((S5_NT, 2, nb, S5_SL), F32)),
        scratch_shapes=[pltpu.VMEM((S5_KT, blk, 2 * S5_SL), F32), pltpu.VMEM((S5_KT, 2, nb, S5_SL), F32)],
        compiler_params=_params("parallel", "arbitrary"),
        name="s5_scan",
    )(u, wb, lam, wc, x0)


def _s5_out_kernel(yf_ref, yb_ref, u_ref, d_ref, w_ref, o_ref):
    y = yf_ref[0] + yb_ref[0] + d_ref[...] * u_ref[0]
    a = jax.nn.gelu(y).astype(BF16)
    val = jnp.dot(a, w_ref[:, 0:D_MODEL], preferred_element_type=F32)
    gate = jnp.dot(a, w_ref[:, D_MODEL:2 * D_MODEL], preferred_element_type=F32)
    o_ref[0] = (val * jax.nn.sigmoid(gate)).astype(o_ref.dtype)


def _s5_out(y_f, y_b, u, d_skip, w_glu, *, tm):
    _, rows, d = u.shape
    row_spec = pl.BlockSpec((1, tm, d), lambda i: (0, i, 0))
    return pl.pallas_call(
        _s5_out_kernel,
        grid=(rows // tm,),
        in_specs=[row_spec, row_spec, row_spec,
                  pl.BlockSpec((1, d), lambda i: (0, 0)),
                  pl.BlockSpec((d, 2 * d), lambda i: (0, 0))],
        out_specs=row_spec,
        out_shape=jax.ShapeDtypeStruct((1, rows, d), BF16),
        compiler_params=_params("parallel"),
        name="s5_out",
    )(y_f, y_b, u, d_skip.astype(F32).reshape(1, d), w_glu.astype(BF16))


def _s5_mixer(u_lat, u_ctx, nb, lam_re, lam_im, log_dt, b_re, b_im, c_re, c_im, d_skip, w_glu):
    lbr, lbi, bbr, bbi = _s5_discretise(lam_re, lam_im, log_dt, b_re, b_im)
    eye = jnp.eye(S5_TILE_G, dtype=F32)
    bb = jnp.stack([bbr, bbi], axis=1).reshape(2, 2, S5_GROUP, S5_NT, S5_TILE_G, S5_STATE)
    wb = jnp.einsum('dzcktp,ts->dktczsp', bb, eye).reshape(2, S5_NT, LANES, 2 * S5_SL).astype(BF16)
    cc = jnp.stack([c_re.astype(F32), -c_im.astype(F32)], axis=0).reshape(2, S5_NT, S5_TILE_G, S5_GROUP, S5_STATE)
    wc = jnp.einsum('zktcp,ts->kzsptc', cc, eye).reshape(S5_NT, 2 * S5_SL, LANES).astype(BF16)
    lam = jnp.stack([lbr, lbi], axis=1).reshape(2, 2, S5_NT, 1, S5_SL).transpose(0, 2, 1, 3, 4)
    x0 = jnp.zeros((S5_NT, 2, nb, S5_SL), F32)
    yc_f, x_f = _s5_scan(u_ctx, wb[0], lam[0], wc, x0, reverse=False, nb=nb)
    yc_b, x_b = _s5_scan(u_ctx, wb[1], lam[1], wc, x0, reverse=True, nb=nb)
    yl_f, _ = _s5_scan(u_lat, wb[0], lam[0], wc, x_f, reverse=False, nb=nb)
    yl_b, _ = _s5_scan(u_lat, wb[1], lam[1], wc, x_b, reverse=True, nb=nb)
    y_lat = _s5_out(yl_f, yl_b, u_lat, d_skip, w_glu, tm=512)
    y_ctx = _s5_out(yc_f, yc_b, u_ctx, d_skip, w_glu, tm=512)
    return y_lat, y_ctx


_NT_DIMS = (((1,), (1,)), ((), ()))
_TN_DIMS = (((0,), (0,)), ((), ()))


def _tri_mask(n, reverse):
    r = lax.broadcasted_iota(jnp.int32, (n, n), 0)
    c = lax.broadcasted_iota(jnp.int32, (n, n), 1)
    return (c >= r) if reverse else (c <= r)


def _cumsum_f32(mask_bf16, g):
    g1 = g.astype(BF16)
    r1 = g - g1.astype(F32)
    g2 = r1.astype(BF16)
    g3 = (r1 - g2.astype(F32)).astype(BF16)
    dot = lambda t: jnp.dot(mask_bf16, t, preferred_element_type=F32)
    return dot(g1) + dot(g2) + dot(g3)


def _gla_scan_kernel(q_ref, k_ref, v_ref, glr_ref, wg_ref, bg_ref, s0_ref, o_ref, s_ref, *, reverse):
    @pl.when(pl.program_id(1) == 0)
    def _():
        s_ref[...] = s0_ref[...]

    nb, ch = q_ref.shape[0], q_ref.shape[1]
    mask = _tri_mask(ch, reverse)
    mask_bf16 = mask.astype(BF16)
    end = 0 if reverse else ch - 1
    for r in range(nb):
        pre = jnp.dot(glr_ref[r].astype(BF16), wg_ref[...], preferred_element_type=F32) + bg_ref[...]
        g = jax.nn.log_sigmoid(pre) / GLA_GATE_NORM
        b = _cumsum_f32(mask_bf16, g)
        b_end = b[end:end + 1, :]
        q_dec = (q_ref[r].astype(F32) * GLA_DK ** -0.5 * jnp.exp(b)).astype(BF16)
        k = k_ref[r].astype(F32)
        k_dec = (k * jnp.exp(-b)).astype(BF16)
        k_end = (k * jnp.exp(b_end - b)).astype(BF16)
        decay_end = jnp.exp(b_end)
        for h in range(GLA_HEADS):
            sk = slice(h * GLA_DK, (h + 1) * GLA_DK)
            sv = slice(h * GLA_DV, (h + 1) * GLA_DV)
            v = v_ref[r, :, sv]
            st = s_ref[r, h]
            scores = lax.dot_general(q_dec[:, sk], k_dec[:, sk], _NT_DIMS, preferred_element_type=F32)
            scores = jnp.where(mask, scores, 0.0).astype(BF16)
            o = (jnp.dot(scores, v, preferred_element_type=F32)
                 + lax.dot_general(q_dec[:, sk], st.astype(BF16), _NT_DIMS, preferred_element_type=F32))
            o_ref[r, :, sv] = o.astype(o_ref.dtype)
            s_ref[r, h] = decay_end[:, sk] * st + lax.dot_general(v, k_end[:, sk], _TN_DIMS,
                                                                   preferred_element_type=F32)


def _gla_scan(z, glr, wg, bg, s0, *, reverse, ch):
    bsz, length, _ = z.shape
    nc = length // ch
    nb = GLA_ROWS if bsz % GLA_ROWS == 0 else 1
    cix = (lambda i: nc - 1 - i) if reverse else (lambda i: i)
    st_spec = pl.BlockSpec((nb, GLA_HEADS, GLA_DV, GLA_DK), lambda b, i: (b, 0, 0, 0))
    shp = jax.ShapeDtypeStruct
    return pl.pallas_call(
        functools.partial(_gla_scan_kernel, reverse=reverse),
        grid=(bsz // nb, nc),
        in_specs=[pl.BlockSpec((nb, ch, GLA_QK), lambda b, i: (b, cix(i), 0)),
                  pl.BlockSpec((nb, ch, GLA_QK), lambda b, i: (b, cix(i), 1)),
                  pl.BlockSpec((nb, ch, GLA_VD), lambda b, i: (b, cix(i), 1)),
                  pl.BlockSpec((nb, ch, LANES), lambda b, i: (b, cix(i), 0)),
                  pl.BlockSpec((LANES, GLA_QK), lambda b, i: (0, 0)),
                  pl.BlockSpec((1, GLA_QK), lambda b, i: (0, 0)),
                  st_spec],
        out_specs=[pl.BlockSpec((nb, ch, GLA_VD), lambda b, i: (b, cix(i), 0)), st_spec],
        out_shape=[shp((bsz, length, GLA_VD), BF16), shp(s0.shape, F32)],
        compiler_params=_params("parallel", "arbitrary"),
        name="gla_scan",
    )(z, z, z, glr, wg, bg, s0)


def _gla_out_kernel(of_ref, ob_ref, r_ref, ng_ref, o_ref):
    for h in range(GLA_HEADS):
        sv = slice(h * GLA_DV, (h + 1) * GLA_DV)
        o = of_ref[0, :, sv].astype(F32) + ob_ref[0, :, sv].astype(F32)
        o = _rms(o, ng_ref[...])
        o_ref[0, :, sv] = (o * _silu(r_ref[0, :, sv].astype(F32))).astype(o_ref.dtype)


def _gla_out(o_f, o_b, z, norm_g, *, tm):
    bsz, length, _ = o_f.shape
    seq = pl.BlockSpec((1, tm, GLA_VD), lambda b, i: (b, i, 0))
    return pl.pallas_call(
        _gla_out_kernel,
        grid=(bsz, length // tm),
        in_specs=[seq, seq, pl.BlockSpec((1, tm, D_MODEL), lambda b, i: (b, i, 2)),
                  pl.BlockSpec((1, GLA_DV), lambda b, i: (0, 0))],
        out_specs=seq,
        out_shape=jax.ShapeDtypeStruct((bsz, length, GLA_VD), BF16),
        compiler_params=_params("parallel", "parallel"),
        name="gla_out",
    )(o_f, o_b, z, norm_g.astype(F32).reshape(1, GLA_DV))


def _gla_weights(w_in, w_gate, b_gate):
    n_main = 2 * GLA_QK + GLA_VD + D_MODEL
    w_main = w_in[:, :n_main].astype(BF16)
    w_glr = jnp.pad(w_in[:, n_main:], ((0, 0), (0, LANES - 2 * GLA_RANK))).astype(BF16)
    wg = jnp.zeros((2, LANES, GLA_QK), F32)
    wg = wg.at[0, :GLA_RANK].set(w_gate[0].astype(F32)).at[1, GLA_RANK:2 * GLA_RANK].set(w_gate[1].astype(F32))
    return w_main, w_glr, wg.astype(BF16), b_gate.astype(F32).reshape(2, 1, GLA_QK)


def _gla_mixer(z_lat, glr_lat, z_ctx, glr_ctx, wg, bg, norm_g):
    bsz = z_lat.shape[0]
    s0 = jnp.zeros((bsz, GLA_HEADS, GLA_DV, GLA_DK), F32)

    def bidirectional(z, glr, s_f, s_b, tm):
        o_f, s_f = _gla_scan(z, glr, wg[0], bg[0], s_f, reverse=False, ch=GLA_CHUNK)
        o_b, s_b = _gla_scan(z, glr, wg[1], bg[1], s_b, reverse=True, ch=GLA_CHUNK)
        return _gla_out(o_f, o_b, z, norm_g, tm=tm), s_f, s_b

    t_ctx, s_f, s_b = bidirectional(z_ctx, glr_ctx, s0, s0, z_ctx.shape[1])
    t_lat, _, _ = bidirectional(z_lat, glr_lat, s_f, s_b, 512)
    return t_lat, t_ctx


def _dot3(a, b):
    a1 = a.astype(BF16)
    a2 = (a - a1.astype(F32)).astype(BF16)
    b1 = b.astype(BF16)
    b2 = (b - b1.astype(F32)).astype(BF16)
    dot = lambda s, t: jnp.dot(s, t, preferred_element_type=F32)
    return dot(a1, b1) + dot(a1, b2) + dot(a2, b1)


def _hy_filter_kernel(bands_ref, w10_ref, w1c_ref, w1s_ref, b1_ref, w2_ref, b2_ref, w3_ref, fr_ref, dl_ref,
                      ha_ref, hb_ref, *, length):
    tm = ha_ref.shape[0]
    n = (pl.program_id(0) * tm + lax.broadcasted_iota(jnp.int32, (tm, 1), 0)).astype(F32)

    def mlp(pos):
        t = pos * (1.0 / length)
        ang = (2.0 * math.pi * t) * bands_ref[...]
        pre = (t * w10_ref[...] + _dot3(jnp.cos(ang), w1c_ref[...]) - _dot3(jnp.sin(ang), w1s_ref[...])
               + b1_ref[...])
        z = jnp.sin(fr_ref[...] * pre)
        z = jnp.sin(fr_ref[...] * (_dot3(z, w2_ref[...]) + b2_ref[...]))
        return z, jnp.exp(-t * dl_ref[...])

    z, dec = mlp(n)
    h_f = _dot3(z, w3_ref[:, 0:D_MODEL]) * dec
    h_b0 = _dot3(z, w3_ref[:, D_MODEL:2 * D_MODEL]) * dec
    ha_ref[...] = (h_f + jnp.where(n == 0.0, h_b0, 0.0)).astype(ha_ref.dtype)
    zb, decb = mlp(length - n)
    h_b = _dot3(zb, w3_ref[:, D_MODEL:2 * D_MODEL]) * decb
    hb_ref[...] = jnp.where(n > 0.0, h_b, 0.0).astype(hb_ref.dtype)


def _hy_filter(length, w1, b1, w2, b2, w3, freq):
    f32 = lambda t: t.astype(F32)
    nb = HY_BANDS
    bands = jnp.pad(jnp.linspace(1e-4, nb - 1, nb, dtype=F32), (0, LANES - nb)).reshape(1, LANES)
    w1 = f32(w1)
    w1c = jnp.pad(w1[1:1 + nb], ((0, LANES - nb), (0, 0)))
    w1s = jnp.pad(w1[1 + nb:1 + 2 * nb], ((0, LANES - nb), (0, 0)))
    log_target = math.log(HY_DECAY_TARGET)
    deltas = jnp.abs(jnp.linspace(log_target / HY_SLOW_DECAY, log_target / HY_FAST_DECAY, D_MODEL, dtype=F32))
    tm = min(length, 512)
    args = [bands, w1[0:1], w1c, w1s, f32(b1).reshape(1, -1), f32(w2), f32(b2).reshape(1, -1), f32(w3),
            f32(freq).reshape(1, -1), deltas.reshape(1, D_MODEL)]
    out_spec = pl.BlockSpec((tm, D_MODEL), lambda i: (i, 0))
    return pl.pallas_call(
        functools.partial(_hy_filter_kernel, length=length),
        grid=(length // tm,),
        in_specs=[pl.BlockSpec(a.shape, lambda i: (0, 0)) for a in args],
        out_specs=[out_spec, out_spec],
        out_shape=[jax.ShapeDtypeStruct((length, D_MODEL), BF16)] * 2,
        compiler_params=_params("parallel"),
        name="hy_filter",
    )(*args)


def _conv3(x_ref, xp_ref, xn_ref, cw_ref, cb_ref, r, nr):
    tm = x_ref.shape[1]
    x = x_ref[0].astype(F32)
    prev_row = jnp.where(r > 0, xp_ref[0][HALO - 1:HALO, :].astype(F32), 0.0)
    next_row = jnp.where(r < nr - 1, xn_ref[0][0:1, :].astype(F32), 0.0)
    rows = lax.broadcasted_iota(jnp.int32, (tm, 1), 0)
    x_prev = jnp.where(rows == 0, prev_row, pltpu.roll(x, 1, axis=0))
    x_next = jnp.where(rows == tm - 1, next_row, pltpu.roll(x, tm - 1, axis=0))
    return cw_ref[0:1, :] * x_prev + cw_ref[1:2, :] * x + cw_ref[2:3, :] * x_next + cb_ref[...]


def _conv3_specs(tm, length, ct, lane_block):
    hb, nhb = tm // HALO, length // HALO
    return [pl.BlockSpec((1, tm, ct), lambda b, r, j: (b, r, lane_block(j))),
            pl.BlockSpec((1, HALO, ct), lambda b, r, j: (b, jnp.maximum(r * hb - 1, 0), lane_block(j))),
            pl.BlockSpec((1, HALO, ct), lambda b, r, j: (b, jnp.minimum((r + 1) * hb, nhb - 1), lane_block(j))),
            pl.BlockSpec((3, ct), lambda b, r, j: (0, lane_block(j))),
            pl.BlockSpec((1, ct), lambda b, r, j: (0, lane_block(j)))]


def _hy_conv_kernel(*refs):
    r, nr = pl.program_id(1), pl.num_programs(1)
    u = [_conv3(*refs[5 * a:5 * a + 5], r, nr) for a in range(3)]
    x0_ref, p_ref = refs[15:]
    x0_ref[0] = u[0].astype(x0_ref.dtype)
    p_ref[0] = (u[2] * u[1]).astype(p_ref.dtype)


def _hy_conv(z, conv_w, conv_b, *, tm):
    bsz, length, _ = z.shape
    nj = D_MODEL // HY_CT
    cw, cb = conv_w.astype(F32), conv_b.astype(F32).reshape(1, -1)
    in_specs, args = [], []
    for a in range(3):
        in_specs += _conv3_specs(tm, length, HY_CT, lambda j, a=a: a * nj + j)
        args += [z, z, z, cw, cb]
    out_spec = pl.BlockSpec((1, tm, HY_CT), lambda b, r, j: (b, r, j))
    return pl.pallas_call(
        _hy_conv_kernel,
        grid=(bsz, length // tm, nj),
        in_specs=in_specs,
        out_specs=[out_spec, out_spec],
        out_shape=[jax.ShapeDtypeStruct((bsz, length, D_MODEL), BF16)] * 2,
        compiler_params=_params("parallel", "parallel", "parallel"),
        name="hy_conv",
    )(*args)


def _dft_cs(rows, cols, modulus):
    m = (rows.astype(jnp.int32)[:, None] * cols.astype(jnp.int32)[None, :]) % modulus
    theta = m.astype(F32) * (2.0 * math.pi / modulus)
    return jnp.cos(theta), jnp.sin(theta)


def _cplx_rows(c, s):
    return jnp.concatenate([jnp.concatenate([c, s], axis=1), jnp.concatenate([-s, c], axis=1)], axis=0)


def _hy_s1_kernel(*refs, real):
    if real:
        zr_ref, f_ref, o_ref = refs
        z = zr_ref[0]
    else:
        zr_ref, zi_ref, f_ref, o_ref = refs
        z = jnp.concatenate([zr_ref[0], zi_ref[0]], axis=0)
    res = jnp.dot(f_ref[...], z, preferred_element_type=F32)
    o_ref[0, 0] = res[:HY_N1].astype(o_ref.dtype)
    o_ref[0, 1] = res[HY_N1:].astype(o_ref.dtype)


def _hy_stage1(p, f, *, real):
    nseq, length, d = p.shape
    n1h = HY_N1 // 2
    cols = (length // n1h) * d
    pv = p.reshape(nseq, n1h, cols)
    npair = nseq if real else nseq // 2
    tc = HY_TC
    in_specs = [pl.BlockSpec((1, n1h, tc), lambda q, j: (q, 0, j))]
    args = [pv]
    if not real:
        in_specs.append(pl.BlockSpec((1, n1h, tc), lambda q, j: (q + npair, 0, j)))
        args.append(pv)
    in_specs.append(pl.BlockSpec(f.shape, lambda q, j: (0, 0)))
    return pl.pallas_call(
        functools.partial(_hy_s1_kernel, real=real),
        grid=(npair, cols // tc),
        in_specs=in_specs,
        out_specs=pl.BlockSpec((1, 2, HY_N1, tc), lambda q, j: (q, 0, 0, j)),
        out_shape=jax.ShapeDtypeStruct((npair, 2, HY_N1, cols), BF16),
        compiler_params=_params("parallel", "parallel"),
        name="hy_stage1",
    )(*args, f)


def _hy_s2f_kernel(a_ref, g_ref, o_ref):
    a = jnp.concatenate([a_ref[0, 0, 0], a_ref[0, 1, 0]], axis=0)
    o_ref[0, 0] = jnp.dot(g_ref[0], a, preferred_element_type=F32)


def _hy_stage2_filter(a, g):
    nseq = a.shape[0]
    av = a.reshape(nseq, 2, HY_N1, HY_N2, D_MODEL)
    return pl.pallas_call(
        _hy_s2f_kernel,
        grid=(HY_N1, nseq),
        in_specs=[pl.BlockSpec((1, 2, 1, HY_N2, D_MODEL), lambda k, q: (q, 0, k, 0, 0)),
                  pl.BlockSpec((1, 2 * HY_N2, 2 * HY_N2), lambda k, q: (k, 0, 0))],
        out_specs=pl.BlockSpec((1, 1, 2 * HY_N2, D_MODEL), lambda k, q: (q, k, 0, 0)),
        out_shape=jax.ShapeDtypeStruct((nseq, HY_N1, 2 * HY_N2, D_MODEL), F32),
        compiler_params=_params("parallel", "parallel"),
        name="hy_stage2_filter",
    )(av, g)


def _cmul(x, h, n):
    xr, xi, hr, hi = x[:n], x[n:], h[:n], h[n:]
    return jnp.concatenate([xr * hr - xi * hi, xr * hi + xi * hr], axis=0)


def _hy_s2_kernel(a_ref, g_ref, gi_ref, h_ref, o_ref):
    a = jnp.concatenate([a_ref[0, 0, 0], a_ref[0, 1, 0]], axis=0)
    x = jnp.dot(g_ref[0], a, preferred_element_type=F32)
    sign = (1 - 2 * (pl.program_id(0) % 2)).astype(F32)
    y = _cmul(x, h_ref[0, 0] + sign * h_ref[1, 0], HY_N2).astype(BF16)
    b = jnp.dot(gi_ref[0], y, preferred_element_type=F32)
    o_ref[0, 0, 0] = b[:HY_N2].astype(o_ref.dtype)
    o_ref[0, 1, 0] = b[HY_N2:].astype(o_ref.dtype)


def _hy_stage2(a, g, gi, hspec):
    npair = a.shape[0]
    av = a.reshape(npair, 2, HY_N1, HY_N2, D_MODEL)
    blk = pl.BlockSpec((1, 2, 1, HY_N2, D_MODEL), lambda k, q: (q, 0, k, 0, 0))
    mat = pl.BlockSpec((1, 2 * HY_N2, 2 * HY_N2), lambda k, q: (k, 0, 0))
    out = pl.pallas_call(
        _hy_s2_kernel,
        grid=(HY_N1, npair),
        in_specs=[blk, mat, mat,
                  pl.BlockSpec((2, 1, 2 * HY_N2, D_MODEL), lambda k, q: (0, k, 0, 0))],
        out_specs=blk,
        out_shape=jax.ShapeDtypeStruct(av.shape, BF16),
        compiler_params=_params("parallel", "parallel"),
        name="hy_stage2",
    )(av, g, gi, hspec)
    return out.reshape(a.shape)


def _hy_s1inv_kernel(b_ref, f_ref, x0_ref, p_ref, bias_ref, o_ref):
    bc = jnp.concatenate([b_ref[0, 0], b_ref[0, 1]], axis=0)
    y = jnp.dot(f_ref[...], bc, preferred_element_type=F32)
    n1h = HY_N1 // 2
    for half in range(2):
        conv = y[half * n1h:(half + 1) * n1h]
        u = p_ref[half, 0].astype(F32)
        o_ref[half, 0] = (x0_ref[half, 0].astype(F32) * (conv + u * bias_ref[...])).astype(o_ref.dtype)


def _hy_stage1_inv(b, f, x0, p, bias):
    npair, _, _, cols = b.shape
    bsz, length, d = x0.shape
    n1h = HY_N1 // 2
    tc = HY_TC
    view = lambda t: t.reshape(2, npair, n1h, cols)
    seq = pl.BlockSpec((2, 1, n1h, tc), lambda q, j: (0, q, 0, j))
    out = pl.pallas_call(
        _hy_s1inv_kernel,
        grid=(npair, cols // tc),
        in_specs=[pl.BlockSpec((1, 2, HY_N1, tc), lambda q, j: (q, 0, 0, j)),
                  pl.BlockSpec(f.shape, lambda q, j: (0, 0)),
                  seq, seq,
                  pl.BlockSpec((1, tc), lambda q, j: (0, 0))],
        out_specs=seq,
        out_shape=jax.ShapeDtypeStruct((2, npair, n1h, cols), BF16),
        compiler_params=_params("parallel", "parallel"),
        name="hy_stage1_inv",
    )(b, f, view(x0), view(p), jnp.tile(bias.astype(F32).reshape(1, d), (1, tc // d)))
    return out.reshape(bsz, length, d)


def _hy_ctx_filter_kernel(ha_ref, hb_ref, f_ref, o_ref):
    h2 = jnp.concatenate([ha_ref[...], hb_ref[...]], axis=0)
    o_ref[...] = jnp.dot(f_ref[...], h2, preferred_element_type=F32)


def _hy_ctx_kernel(zr_ref, zi_ref, f_ref, fi_ref, h_ref, x0_ref, p_ref, bias_ref, o_ref):
    length = zr_ref.shape[1]
    z = jnp.concatenate([zr_ref[0], zi_ref[0]], axis=0)
    x = jnp.dot(f_ref[...], z, preferred_element_type=F32)
    y = _cmul(x, h_ref[...], 2 * length).astype(BF16)
    out = jnp.dot(fi_ref[...], y, preferred_element_type=F32)
    for half in range(2):
        conv = out[half * length:(half + 1) * length]
        u = p_ref[half, 0].astype(F32)
        o_ref[half, 0] = (x0_ref[half, 0].astype(F32) * (conv + u * bias_ref[...])).astype(o_ref.dtype)


def _hy_ctx_conv(x0, p, ha, hb, bias):
    bsz, length, d = p.shape
    npair = bsz // 2
    n = 2 * length
    k = jnp.arange(n)
    c, s = _dft_cs(k, jnp.arange(length), n)
    f = _cplx_rows(c, s).astype(BF16)
    fi = (_cplx_rows(c.T, -s.T) * (1.0 / n)).astype(BF16)
    cf, sf = _dft_cs(k, k, n)
    ffilt = jnp.concatenate([cf, -sf], axis=0).astype(BF16)
    full = lambda t: pl.BlockSpec(t.shape, lambda *_: (0,) * t.ndim)
    hspec = pl.pallas_call(
        _hy_ctx_filter_kernel,
        out_shape=jax.ShapeDtypeStruct((2 * n, d), F32),
        compiler_params=_params(),
        name="hy_ctx_filter",
    )(ha, hb, ffilt)
    view = lambda t: t.reshape(2, npair, length, d)
    seq = pl.BlockSpec((2, 1, length, d), lambda q: (0, q, 0, 0))
    bias2 = bias.astype(F32).reshape(1, d)
    out = pl.pallas_call(
        _hy_ctx_kernel,
        grid=(npair,),
        in_specs=[pl.BlockSpec((1, length, d), lambda q: (q, 0, 0)),
                  pl.BlockSpec((1, length, d), lambda q: (q + npair, 0, 0)),
                  full(f), full(fi), full(hspec), seq, seq, full(bias2)],
        out_specs=seq,
        out_shape=jax.ShapeDtypeStruct((2, npair, length, d), BF16),
        compiler_params=_params("parallel"),
        name="hy_ctx_conv",
    )(p, p, f, fi, hspec, view(x0), view(p), bias2)
    return out.reshape(bsz, length, d)


def _hy_lat_conv(x0, p, ha, hb, bias):
    bsz, length, d = p.shape
    assert 2 * length == HY_N1 * HY_N2 and d == D_MODEL and bsz % 2 == 0
    n = 2 * length
    k1, n1 = jnp.arange(HY_N1), jnp.arange(HY_N1 // 2)
    c1, s1 = _dft_cs(k1, n1, HY_N1)
    f1 = _cplx_rows(c1, s1).astype(BF16)
    f1_real = jnp.concatenate([c1, -s1], axis=0).astype(BF16)
    f1_inv = (_cplx_rows(c1.T, -s1.T) * (1.0 / n)).astype(BF16)
    k = (k1[:, None] + HY_N1 * jnp.arange(HY_N2)[None, :]).reshape(-1)
    c2, s2 = _dft_cs(k, jnp.arange(HY_N2), n)
    c2, s2 = c2.reshape(HY_N1, HY_N2, HY_N2), s2.reshape(HY_N1, HY_N2, HY_N2)
    g = jax.vmap(_cplx_rows)(c2, s2).astype(BF16)
    gi = jax.vmap(_cplx_rows)(c2.transpose(0, 2, 1), -s2.transpose(0, 2, 1)).astype(BF16)
    hfilt = jnp.stack([ha, hb], axis=0)
    hspec = _hy_stage2_filter(_hy_stage1(hfilt, f1_real, real=True), g)
    a = _hy_stage1(p, f1, real=False)
    b = _hy_stage2(a, g, gi, hspec)
    return _hy_stage1_inv(b, f1_inv, x0, p, bias)


def _hy_mixer(z_lat, z_ctx, conv_w, conv_b, f_w1, f_b1, f_w2, f_b2, f_w3, f_freq, f_bias):
    def run(z, tm, conv_fn):
        length = z.shape[1]
        x0, p = _hy_conv(z, conv_w, conv_b, tm=tm)
        ha, hb = _hy_filter(length, f_w1, f_b1, f_w2, f_b2, f_w3, f_freq)
        return conv_fn(x0, p, ha, hb, f_bias)

    return run(z_lat, 512, _hy_lat_conv), run(z_ctx, z_ctx.shape[1], _hy_ctx_conv)


def _ml_qkv_kernel(xm_ref, xp_ref, xn_ref, cw_ref, cb_ref, wq_ref, wk_ref, wv_ref, wg_ref, bg_ref,
                   q_ref, k_ref, v_ref, xc_ref, g_ref):
    j = pl.program_id(2)
    xm = xm_ref[0]
    conv = _conv3(xm_ref, xp_ref, xn_ref, cw_ref, cb_ref, pl.program_id(1), pl.num_programs(1))
    xc = _silu(conv).astype(BF16)
    q = jnp.dot(xc, wq_ref[0], preferred_element_type=F32).astype(BF16)
    k = jnp.dot(xc, wk_ref[0], preferred_element_type=F32).astype(BF16)
    v = jnp.dot(xm, wv_ref[0], preferred_element_type=F32).astype(BF16)
    q_ref[0], k_ref[0], v_ref[0], xc_ref[0] = q, k, v, xc
    contrib = (jnp.dot(q, wg_ref[0], preferred_element_type=F32)
               + jnp.dot(k, wg_ref[1], preferred_element_type=F32)
               + jnp.dot(v, wg_ref[2], preferred_element_type=F32))

    @pl.when(j == 0)
    def _():
        g_ref[0] = contrib

    @pl.when(j > 0)
    def _():
        g_ref[0] += contrib

    @pl.when(j == pl.num_programs(2) - 1)
    def _():
        g = g_ref[0] + bg_ref[...]
        lane = lax.broadcasted_iota(jnp.int32, g.shape, 1)
        g_ref[0] = jnp.where(lane % (2 * ML_HEADS) >= ML_HEADS, jax.nn.log_sigmoid(g), g)


def _ml_qkv(z, conv_w, conv_b, wq, wk, wv, wg, bg, *, tm):
    bsz, length, _ = z.shape
    nj = ML_INNER // ML_CT
    seq = lambda: pl.BlockSpec((1, tm, ML_CT), lambda b, r, j: (b, r, j))
    shp = jax.ShapeDtypeStruct
    return pl.pallas_call(
        _ml_qkv_kernel,
        grid=(bsz, length // tm, nj),
        in_specs=_conv3_specs(tm, length, ML_CT, lambda j: j) + [
                  pl.BlockSpec((1, ML_CT, ML_CT), lambda b, r, j: (j, 0, 0)),
                  pl.BlockSpec((1, ML_CT, ML_CT), lambda b, r, j: (j, 0, 0)),
                  pl.BlockSpec((1, ML_CT, ML_CT), lambda b, r, j: (j, 0, 0)),
                  pl.BlockSpec((3, ML_CT, LANES), lambda b, r, j: (0, j, 0)),
                  pl.BlockSpec((1, LANES), lambda b, r, j: (0, 0))],
        out_specs=[seq(), seq(), seq(), seq(),
                   pl.BlockSpec((1, tm, LANES), lambda b, r, j: (b, r, 0))],
        out_shape=[shp((bsz, length, ML_INNER), BF16)] * 4 + [shp((bsz, length, LANES), F32)],
        compiler_params=_params("parallel", "parallel", "arbitrary"),
        name="ml_qkv",
    )(z, z, z, conv_w, conv_b, wq, wk, wv, wg, bg)


def _ml_scan_kernel(q_ref, k_ref, v_ref, g_ref, c0_ref, n0_ref, m0_ref, h_ref, c_ref, n_ref, m_ref,
                    *, reverse, z):
    @pl.when(pl.program_id(1) == 0)
    def _():
        c_ref[...] = c0_ref[...]
        n_ref[...] = n0_ref[...]
        m_ref[...] = m0_ref[...]

    ch = q_ref.shape[1]
    mask = _tri_mask(ch, reverse)
    gates = g_ref[0]
    bcum = _cumsum_f32(mask.astype(BF16), gates)
    gates_t, bcum_t = gates.T, bcum.T
    end = 0 if reverse else ch - 1
    for h in range(ML_HEADS):
        li, lf = z * 2 * ML_HEADS + h, z * 2 * ML_HEADS + ML_HEADS + h
        sl = slice(h * ML_DH, (h + 1) * ML_DH)
        q, v = q_ref[0, :, sl], v_ref[0, :, sl]
        k = k_ref[0, :, sl].astype(F32) * ML_DH ** -0.5
        b_col, b_row = bcum[:, lf:lf + 1], bcum_t[lf:lf + 1, :]
        i_col, i_row = gates[:, li:li + 1], gates_t[li:li + 1, :]
        b_end = b_col[end:end + 1, :]
        m_prev = m_ref[0, h][0:1, 0:1]
        d_intra = jnp.where(mask, b_col - b_row + i_row, -jnp.inf)
        d_inter = b_col + m_prev
        m_tok = jnp.maximum(jnp.max(d_intra, axis=1, keepdims=True), d_inter)
        w_inter = jnp.exp(d_inter - m_tok)
        s = lax.dot_general(q, k.astype(BF16), _NT_DIMS, preferred_element_type=F32) * jnp.exp(d_intra - m_tok)
        c_mat = c_ref[0, h]
        n_vec = n_ref[0, h]
        num = (jnp.dot(s.astype(BF16), v, preferred_element_type=F32)
               + w_inter * jnp.dot(q, c_mat.astype(BF16), preferred_element_type=F32))
        den = (jnp.sum(s, axis=1, keepdims=True)
               + w_inter * jnp.sum(q.astype(F32) * n_vec, axis=1, keepdims=True))
        h_ref[0, :, sl] = (num / jnp.maximum(jnp.abs(den), jnp.exp(-m_tok))).astype(h_ref.dtype)
        d_state = b_end - b_col + i_col
        m_new = jnp.maximum(b_end + m_prev, jnp.max(d_state, axis=0, keepdims=True))
        w_prev = jnp.exp(b_end + m_prev - m_new)
        kw = k * jnp.exp(d_state - m_new)
        c_ref[0, h] = w_prev * c_mat + lax.dot_general(kw.astype(BF16), v, _TN_DIMS, preferred_element_type=F32)
        n_ref[0, h] = w_prev * n_vec + jnp.sum(kw, axis=0, keepdims=True)
        m_ref[0, h] = jnp.broadcast_to(m_new, m_ref.shape[2:])


def _ml_scan(q, k, v, gates, state, *, reverse, z, ch):
    bsz, length, _ = q.shape
    nc = length // ch
    cix = (lambda i: nc - 1 - i) if reverse else (lambda i: i)
    seq = lambda w: pl.BlockSpec((1, ch, w), lambda b, i: (b, cix(i), 0))
    st_specs = [pl.BlockSpec((1,) + s.shape[1:], lambda b, i: (b, 0, 0, 0)) for s in state]
    shp = jax.ShapeDtypeStruct
    out = pl.pallas_call(
        functools.partial(_ml_scan_kernel, reverse=reverse, z=z),
        grid=(bsz, nc),
        in_specs=[seq(ML_INNER), seq(ML_INNER), seq(ML_INNER), seq(LANES)] + st_specs,
        out_specs=[seq(ML_INNER)] + st_specs,
        out_shape=[shp((bsz, length, ML_INNER), BF16)] + [shp(s.shape, F32) for s in state],
        compiler_params=_params("parallel", "arbitrary"),
        name="ml_scan",
    )(q, k, v, gates, *state)
    return out[0], tuple(out[1:])


def _ml_out_kernel(hf_ref, hb_ref, xc_ref, z_ref, ng_ref, sk_ref, o_ref):
    for h in range(ML_HEADS):
        sl = slice(h * ML_DH, (h + 1) * ML_DH)
        o = hf_ref[0, :, sl].astype(F32) + hb_ref[0, :, sl].astype(F32)
        o = o - jnp.mean(o, axis=-1, keepdims=True)
        o = o * lax.rsqrt(jnp.mean(o * o, axis=-1, keepdims=True) + EPS) * ng_ref[:, sl]
        t = (o + sk_ref[:, sl] * xc_ref[0, :, sl].astype(F32)) * _silu(z_ref[0, :, sl].astype(F32))
        o_ref[0, :, sl] = t.astype(o_ref.dtype)


def _ml_out(h_f, h_b, xc, z, norm_g, skip, *, tm):
    bsz, length, _ = h_f.shape
    seq = pl.BlockSpec((1, tm, ML_INNER), lambda b, i: (b, i, 0))
    vec = pl.BlockSpec((1, ML_INNER), lambda b, i: (0, 0))
    return pl.pallas_call(
        _ml_out_kernel,
        grid=(bsz, length // tm),
        in_specs=[seq, seq, seq, pl.BlockSpec((1, tm, ML_INNER), lambda b, i: (b, i, 1)), vec, vec],
        out_specs=seq,
        out_shape=jax.ShapeDtypeStruct((bsz, length, ML_INNER), BF16),
        compiler_params=_params("parallel", "parallel"),
        name="ml_out",
    )(h_f, h_b, xc, z, norm_g.astype(F32).reshape(1, ML_INNER), skip.astype(F32).reshape(1, ML_INNER))


def _ml_weights(w_q, w_k, w_v, w_gates, b_gates):
    nb = ML_CT // ML_BLOCK
    eye = jnp.eye(nb, dtype=F32)

    def dense(w):
        w = w.astype(F32).reshape(ML_INNER // ML_CT, nb, ML_BLOCK, ML_BLOCK)
        return jnp.einsum('tncd,nm->tncmd', w, eye).reshape(ML_INNER // ML_CT, ML_CT, ML_CT).astype(BF16)

    ng = 2 * 2 * ML_HEADS
    wg = w_gates.astype(F32).reshape(2, 3, ML_INNER, 2 * ML_HEADS).transpose(1, 2, 0, 3).reshape(3, ML_INNER, ng)
    wg = jnp.pad(wg, ((0, 0), (0, 0), (0, LANES - ng))).astype(BF16)
    bg = jnp.pad(b_gates.astype(F32).reshape(1, ng), ((0, 0), (0, LANES - ng)))
    return dense(w_q), dense(w_k), dense(w_v), wg, bg


def _ml_mixer(z_lat, z_ctx, conv_w, conv_b, w_q, w_k, w_v, w_gates, b_gates, norm_g, skip):
    wq, wk, wv, wg, bg = _ml_weights(w_q, w_k, w_v, w_gates, b_gates)
    cw, cb = conv_w.astype(F32), conv_b.astype(F32).reshape(1, ML_INNER)
    bsz = z_lat.shape[0]
    state = (jnp.zeros((bsz, ML_HEADS, ML_DH, ML_DH), F32),
             jnp.zeros((bsz, ML_HEADS, 1, ML_DH), F32),
             jnp.full((bsz, ML_HEADS, 8, LANES), -jnp.inf, F32))

    def bidirectional(z, st_f, st_b, tm):
        q, k, v, xc, gates = _ml_qkv(z, cw, cb, wq, wk, wv, wg, bg, tm=tm)
        h_f, st_f = _ml_scan(q, k, v, gates, st_f, reverse=False, z=0, ch=ML_CHUNK)
        h_b, st_b = _ml_scan(q, k, v, gates, st_b, reverse=True, z=1, ch=ML_CHUNK)
        return _ml_out(h_f, h_b, xc, z, norm_g, skip, tm=tm), st_f, st_b

    t_ctx, st_f, st_b = bidirectional(z_ctx, state, state, z_ctx.shape[1])
    t_lat, _, _ = bidirectional(z_lat, st_f, st_b, 512)
    return t_lat, t_ctx


def kernel(x, c, ctx, c_ctx, mod_w, mod_b, norm_g, ffn_w_in, ffn_w_out, gla_w_in, gla_w_gate, gla_b_gate, gla_norm_g, gla_w_out, hy_w_in, hy_conv_w, hy_conv_b, hy_f_w1, hy_f_b1, hy_f_w2, hy_f_b2, hy_f_w3, hy_f_freq, hy_f_bias, hy_w_out, ml_w_in, ml_conv_w, ml_conv_b, ml_w_q, ml_w_k, ml_w_v, ml_w_gates, ml_b_gates, ml_norm_g, ml_skip, ml_w_out, s5_lam_re, s5_lam_im, s5_log_dt, s5_b_re, s5_b_im, s5_c_re, s5_c_im, s5_d, s5_w_glu):
    bsz, seq, d = x.shape
    ctx_len = ctx.shape[1]
    depth = mod_w.shape[0]
    mods = _modulation(c, c_ctx, mod_w, mod_b)
    lat = x.astype(F32)
    cx = ctx.astype(F32)
    tm_lat, tm_ctx = 512, ctx_len
    lat_col_major = False
    for i in range(depth):
        kind, j = i % N_MIXERS, i // N_MIXERS
        last = i == depth - 1
        if (kind >= 2) != lat_col_major:
            lat = _from_col_major(lat) if lat_col_major else _to_col_major(lat)
            lat_col_major = not lat_col_major
        mod_l = mods[i, :bsz]
        mod_c = mods[i, bsz:bsz + 1]
        win, wout = _ffn_weights(ffn_w_in[i], ffn_w_out[i])
        wmix = None
        tmajor = False
        if kind == 3:
            tmajor = True
            u_lat = _norm_mod(lat, mod_l, norm_g[i, 0], tm=tm_lat, tmajor=True)
            u_ctx = _norm_mod(cx, mod_c, norm_g[i, 0], tm=tm_ctx, tmajor=True)
            y_lat, y_ctx = _s5_mixer(u_lat.reshape(1, seq * bsz, d), u_ctx.reshape(1, ctx_len * bsz, d), bsz,
                                     s5_lam_re[j], s5_lam_im[j], s5_log_dt[j], s5_b_re[j], s5_b_im[j],
                                     s5_c_re[j], s5_c_im[j], s5_d[j], s5_w_glu[j])
            y_lat = y_lat.reshape(1, seq, bsz * d)
            y_ctx = y_ctx.reshape(1, ctx_len, bsz * d)
        elif kind == 0:
            w_main, w_glr, wg, bg = _gla_weights(gla_w_in[j], gla_w_gate[j], gla_b_gate[j])
            z_lat, glr_lat = _norm_proj(lat, mod_l, norm_g[i, 0], [w_main, w_glr], [BF16, F32], tm=tm_lat)
            z_ctx, glr_ctx = _norm_proj(cx, mod_c, norm_g[i, 0], [w_main, w_glr], [BF16, F32], tm=tm_ctx)
            y_lat, y_ctx = _gla_mixer(z_lat, glr_lat, z_ctx, glr_ctx, wg, bg, gla_norm_g[j])
            wmix = gla_w_out[j].astype(BF16)
        elif kind == 2:
            w_in = ml_w_in[j].astype(BF16)
            (z_lat,) = _norm_proj(lat, mod_l, norm_g[i, 0], [w_in], [BF16], tm=tm_lat)
            (z_ctx,) = _norm_proj(cx, mod_c, norm_g[i, 0], [w_in], [BF16], tm=tm_ctx)
            y_lat, y_ctx = _ml_mixer(z_lat, z_ctx, ml_conv_w[j], ml_conv_b[j], ml_w_q[j], ml_w_k[j], ml_w_v[j],
                                     ml_w_gates[j], ml_b_gates[j], ml_norm_g[j], ml_skip[j])
            wmix = ml_w_out[j].astype(BF16)
        else:
            w_in = hy_w_in[j].astype(BF16)
            (z_lat,) = _norm_proj(lat, mod_l, norm_g[i, 0], [w_in], [BF16], tm=tm_lat)
            (z_ctx,) = _norm_proj(cx, mod_c, norm_g[i, 0], [w_in], [BF16], tm=tm_ctx)
            y_lat, y_ctx = _hy_mixer(z_lat, z_ctx, hy_conv_w[j], hy_conv_b[j], hy_f_w1[j], hy_f_b1[j],
                                     hy_f_w2[j], hy_f_b2[j], hy_f_w3[j], hy_f_freq[j], hy_f_bias[j])
            wmix = hy_w_out[j].astype(BF16)
        lat = _post_ffn(lat, y_lat, mod_l, norm_g[i], wmix, win, wout, tm=tm_lat, tmajor=tmajor)
        if not last:
            cx = _post_ffn(cx, y_ctx, mod_c, norm_g[i], wmix, win, wout, tm=tm_ctx, tmajor=tmajor)
    return _from_col_major(lat) if lat_col_major else lat
```

```python
import functools
import math

import jax
import jax.numpy as jnp
from jax import lax
from jax.experimental import pallas as pl
from jax.experimental.pallas import tpu as pltpu

F32 = jnp.float32
BF16 = jnp.bfloat16

D_MODEL = 1024
GRID_W = 64
N_MOD = 6
EPS = 1e-6
FFN_HIDDEN = 2816
FFN_CHUNK = 256
N_FFN_CHUNKS = FFN_HIDDEN // FFN_CHUNK
MOD_ROWS = 16
VMEM_LIMIT = 56 * 1024 * 1024
LANES = 128
N_MIXERS = 4
PROJ_TN = 512

GLA_HEADS = 4
GLA_QK = D_MODEL // 2
GLA_VD = D_MODEL
GLA_DK = GLA_QK // GLA_HEADS
GLA_DV = GLA_VD // GLA_HEADS
GLA_RANK = 16
GLA_GATE_NORM = 16.0
GLA_CHUNK = 64
GLA_ROWS = 4

HALO = 16
HY_BANDS = 16
HY_DECAY_TARGET = 1e-2
HY_FAST_DECAY = 0.3
HY_SLOW_DECAY = 1.5
HY_CT = 256
HY_N1, HY_N2 = 64, 128
HY_TC = 8192

ML_HEADS = 4
ML_INNER = 2 * D_MODEL
ML_DH = ML_INNER // ML_HEADS
ML_BLOCK = 4
ML_CHUNK = 256
ML_CT = 512
ML_BD = 256

S5_GROUP = 16
S5_GROUPS = D_MODEL // S5_GROUP
S5_STATE = 64
S5_TILE_G = LANES // S5_GROUP
S5_NT = S5_GROUPS // S5_TILE_G
S5_SL = S5_TILE_G * S5_STATE
S5_TB = 64
S5_KT = 4


def _rms(x, g):
    return x * lax.rsqrt(jnp.mean(x * x, axis=-1, keepdims=True) + EPS) * g


def _silu(x):
    return x * jax.nn.sigmoid(x)


def _params(*sem):
    return pltpu.CompilerParams(dimension_semantics=sem, vmem_limit_bytes=VMEM_LIMIT)


def _mod_kernel(cc_ref, w_ref, b_ref, o_ref):
    a = _silu(cc_ref[...])
    o_ref[0] = jnp.dot(a.astype(BF16), w_ref[0].astype(BF16), preferred_element_type=F32) + b_ref[0]


def _modulation(c, c_ctx, mod_w, mod_b):
    depth, d, n = mod_w.shape
    bsz = c.shape[0]
    cc = jnp.concatenate([c.astype(F32), c_ctx.astype(F32)[None],
                          jnp.zeros((MOD_ROWS - bsz - 1, d), F32)], axis=0)
    tn = 1536
    out = pl.pallas_call(
        _mod_kernel,
        grid=(depth, n // tn),
        in_specs=[pl.BlockSpec((MOD_ROWS, d), lambda i, j: (0, 0)),
                  pl.BlockSpec((1, d, tn), lambda i, j: (i, 0, j)),
                  pl.BlockSpec((1, 1, tn), lambda i, j: (i, 0, j))],
        out_specs=pl.BlockSpec((1, MOD_ROWS, tn), lambda i, j: (i, 0, j)),
        out_shape=jax.ShapeDtypeStruct((depth, MOD_ROWS, n), F32),
        compiler_params=_params("parallel", "parallel"),
        name="adaln_mod",
    )(cc, mod_w, mod_b.reshape(depth, 1, n))
    return out.reshape(depth, MOD_ROWS, N_MOD, d)


def _to_col_major(x):
    bsz, length, d = x.shape
    return x.reshape(bsz, length // GRID_W, GRID_W, d).transpose(0, 2, 1, 3).reshape(bsz, length, d)


def _from_col_major(x):
    bsz, length, d = x.shape
    return x.reshape(bsz, GRID_W, length // GRID_W, d).transpose(0, 2, 1, 3).reshape(bsz, length, d)


def _seq_spec(tmajor, tm, width):
    if tmajor:
        return pl.BlockSpec((1, tm, width), lambda b, i: (0, i, b))
    return pl.BlockSpec((1, tm, width), lambda b, i: (b, i, 0))


def _mod_spec(mod):
    if mod.shape[0] > 1:
        return pl.BlockSpec((1, N_MOD, D_MODEL), lambda b, i: (b, 0, 0))
    return pl.BlockSpec((1, N_MOD, D_MODEL), lambda b, i: (0, 0, 0))


def _norm_kernel(x_ref, m_ref, g_ref, o_ref):
    h = _rms(x_ref[0], g_ref[...]) * (1.0 + m_ref[0, 1:2, :]) + m_ref[0, 0:1, :]
    o_ref[0] = h.astype(o_ref.dtype)


def _norm_mod(x, mod, g, *, tm, tmajor=False, out_dtype=F32):
    bsz, length, d = x.shape
    out_shape = (1, length, bsz * d) if tmajor else (bsz, length, d)
    return pl.pallas_call(
        _norm_kernel,
        grid=(bsz, length // tm),
        in_specs=[_seq_spec(False, tm, d), _mod_spec(mod), pl.BlockSpec((1, d), lambda b, i: (0, 0))],
        out_specs=_seq_spec(tmajor, tm, d),
        out_shape=jax.ShapeDtypeStruct(out_shape, out_dtype),
        compiler_params=_params("parallel", "parallel"),
        name="norm_mod",
    )(x, mod, g.reshape(1, d))


def _proj_kernel(*refs, n_w):
    x_ref, m_ref, g_ref = refs[:3]
    w_refs, o_refs = refs[3:3 + n_w], refs[3 + n_w:]
    h = (_rms(x_ref[0], g_ref[...]) * (1.0 + m_ref[0, 1:2, :]) + m_ref[0, 0:1, :]).astype(BF16)
    for w_ref, o_ref in zip(w_refs, o_refs):
        n = w_ref.shape[1]
        tn = min(n, PROJ_TN)
        for n0 in range(0, n, tn):
            o_ref[0, :, n0:n0 + tn] = jnp.dot(h, w_ref[:, n0:n0 + tn],
                                              preferred_element_type=F32).astype(o_ref.dtype)


def _norm_proj(x, mod, g, ws, dtypes, *, tm):
    bsz, length, d = x.shape
    full = lambda shape: pl.BlockSpec(shape, lambda b, i: (0,) * len(shape))
    return pl.pallas_call(
        functools.partial(_proj_kernel, n_w=len(ws)),
        grid=(bsz, length // tm),
        in_specs=[_seq_spec(False, tm, d), _mod_spec(mod), full((1, d))] + [full(w.shape) for w in ws],
        out_specs=[_seq_spec(False, tm, w.shape[1]) for w in ws],
        out_shape=[jax.ShapeDtypeStruct((bsz, length, w.shape[1]), dt) for w, dt in zip(ws, dtypes)],
        compiler_params=_params("parallel", "parallel"),
        name="norm_proj",
    )(x, mod, g.reshape(1, d), *ws)


def _ffn_kernel(*refs, has_wmix):
    if has_wmix:
        x_ref, y_ref, m_ref, g_ref, wmix_ref, win_ref, wout_ref, o_ref = refs
        y = jnp.dot(y_ref[0], wmix_ref[...], preferred_element_type=F32)
    else:
        x_ref, y_ref, m_ref, g_ref, win_ref, wout_ref, o_ref = refs
        y = y_ref[0].astype(F32)
    lat = x_ref[0] + m_ref[0, 2:3, :] * _rms(y, g_ref[1:2, :])
    h = (_rms(lat, g_ref[2:3, :]) * (1.0 + m_ref[0, 4:5, :]) + m_ref[0, 3:4, :]).astype(BF16)

    def body(c, acc):
        gate = jnp.dot(h, win_ref[0, c], preferred_element_type=F32)
        up = jnp.dot(h, win_ref[1, c], preferred_element_type=F32)
        a = (_silu(gate) * up).astype(BF16)
        return acc + jnp.dot(a, wout_ref[c], preferred_element_type=F32)

    acc = lax.fori_loop(0, N_FFN_CHUNKS, body, jnp.zeros(lat.shape, F32), unroll=True)
    o_ref[0] = lat + m_ref[0, 5:6, :] * _rms(acc, g_ref[3:4, :])


def _post_ffn(x, y, mod, g4, wmix, win, wout, *, tm, tmajor=False):
    bsz, length, d = x.shape
    width = y.shape[-1] // bsz if tmajor else y.shape[-1]
    full = lambda shape: pl.BlockSpec(shape, lambda b, i: (0,) * len(shape))
    in_specs = [_seq_spec(False, tm, d), _seq_spec(tmajor, tm, width), _mod_spec(mod), full((4, d))]
    args = [x, y, mod, g4]
    if wmix is not None:
        in_specs.append(full(wmix.shape))
        args.append(wmix)
    in_specs += [full(win.shape), full(wout.shape)]
    args += [win, wout]
    return pl.pallas_call(
        functools.partial(_ffn_kernel, has_wmix=wmix is not None),
        grid=(bsz, length // tm),
        in_specs=in_specs,
        out_specs=_seq_spec(False, tm, d),
        out_shape=jax.ShapeDtypeStruct(x.shape, F32),
        compiler_params=_params("parallel", "parallel"),
        name="post_ffn",
    )(*args)


def _ffn_weights(w_in, w_out):
    d = w_in.shape[0]
    win = w_in.astype(BF16).reshape(d, 2, N_FFN_CHUNKS, FFN_CHUNK).transpose(1, 2, 0, 3)
    wout = w_out.astype(BF16).reshape(N_FFN_CHUNKS, FFN_CHUNK, d)
    return win, wout


def _s5_disc_kernel(lre_ref, lim_ref, ldt_ref, bre_ref, bim_ref, lbr_ref, lbi_ref, bbr_ref, bbi_ref):
    lre, lim = lre_ref[...], lim_ref[...]
    dt = jnp.exp(ldt_ref[...])
    mag = jnp.exp(lre * dt)
    lbr = mag * jnp.cos(lim * dt)
    lbi = mag * jnp.sin(lim * dt)
    lbr_ref[...] = lbr
    lbi_ref[...] = lbi
    nr = lbr - 1.0
    den = lre * lre + lim * lim
    cr = (nr * lre + lbi * lim) / den
    ci = (lbi * lre - nr * lim) / den
    for dr in range(2):
        bbr_ref[dr] = cr[dr:dr + 1] * bre_ref[...] - ci[dr:dr + 1] * bim_ref[...]
        bbi_ref[dr] = cr[dr:dr + 1] * bim_ref[...] + ci[dr:dr + 1] * bre_ref[...]


def _s5_discretise(lam_re, lam_im, log_dt, b_re, b_im):
    gp = S5_GROUPS * S5_STATE
    flat = lambda t: t.astype(F32).reshape(2, gp)
    ldt = jnp.repeat(log_dt.astype(F32), S5_STATE, axis=1)
    bt = lambda t: t.astype(F32).reshape(gp, S5_GROUP).T
    shp = jax.ShapeDtypeStruct
    return pl.pallas_call(
        _s5_disc_kernel,
        out_shape=(shp((2, gp), F32), shp((2, gp), F32),
                   shp((2, S5_GROUP, gp), F32), shp((2, S5_GROUP, gp), F32)),
        name="s5_discretise",
    )(flat(lam_re), flat(lam_im), ldt, bt(b_re), bt(b_im))


def _s5_scan_kernel(u_ref, wb_ref, lam_ref, wc_ref, x0_ref, y_ref, xt_ref, bu_scr, st_scr, *, reverse, nb):
    i = pl.program_id(1)

    @pl.when(i == 0)
    def _():
        st_scr[...] = x0_ref[...]

    for kt in range(S5_KT):
        u = u_ref[0, :, kt * LANES:(kt + 1) * LANES].astype(BF16)
        bu_scr[kt] = jnp.dot(u, wb_ref[kt], preferred_element_type=F32)
    for kt in range(S5_KT):
        lr = jnp.broadcast_to(lam_ref[kt, 0], (nb, S5_SL))
        li = jnp.broadcast_to(lam_ref[kt, 1], (nb, S5_SL))
        xr, xi = st_scr[kt, 0], st_scr[kt, 1]
        for s in range(S5_TB):
            r0 = ((S5_TB - 1 - s) if reverse else s) * nb
            xr, xi = (lr * xr - li * xi + bu_scr[kt, r0:r0 + nb, 0:S5_SL],
                      lr * xi + li * xr + bu_scr[kt, r0:r0 + nb, S5_SL:2 * S5_SL])
            bu_scr[kt, r0:r0 + nb, 0:S5_SL] = xr
            bu_scr[kt, r0:r0 + nb, S5_SL:2 * S5_SL] = xi
        st_scr[kt, 0] = xr
        st_scr[kt, 1] = xi
        y_ref[0, :, kt * LANES:(kt + 1) * LANES] = jnp.dot(bu_scr[kt].astype(BF16), wc_ref[kt],
                                                            preferred_element_type=F32)

    @pl.when(i == pl.num_programs(1) - 1)
    def _():
        xt_ref[...] = st_scr[...]


def _s5_scan(u, wb, lam, wc, x0, *, reverse, nb):
    _, rows, d = u.shape
    nt = rows // (S5_TB * nb)
    tix = (lambda i: nt - 1 - i) if reverse else (lambda i: i)
    blk = S5_TB * nb
    shp = jax.ShapeDtypeStruct
    return pl.pallas_call(
        functools.partial(_s5_scan_kernel, reverse=reverse, nb=nb),
        grid=(S5_NT // S5_KT, nt),
        in_specs=[pl.BlockSpec((1, blk, S5_KT * LANES), lambda k, i: (0, tix(i), k)),
                  pl.BlockSpec((S5_KT, LANES, 2 * S5_SL), lambda k, i: (k, 0, 0)),
                  pl.BlockSpec((S5_KT, 2, 1, S5_SL), lambda k, i: (k, 0, 0, 0)),
                  pl.BlockSpec((S5_KT, 2 * S5_SL, LANES), lambda k, i: (k, 0, 0)),
                  pl.BlockSpec((S5_KT, 2, nb, S5_SL), lambda k, i: (k, 0, 0, 0))],
        out_specs=[pl.BlockSpec((1, blk, S5_KT * LANES), lambda k, i: (0, tix(i), k)),
                   pl.BlockSpec((S5_KT, 2, nb, S5_SL), lambda k, i: (k, 0, 0, 0))],
        out_shape=(shp((1, rows, d), F32), shp((S5_NT, 2, nb, S5_SL), F32)),
        scratch_shapes=[pltpu.VMEM((S5_KT, blk, 2 * S5_SL), F32), pltpu.VMEM((S5_KT, 2, nb, S5_SL), F32)],
        compiler_params=_params("parallel", "arbitrary"),
        name="s5_scan",
    )(u, wb, lam, wc, x0)


def _s5_out_kernel(yf_ref, yb_ref, u_ref, d_ref, w_ref, o_ref):
    y = yf_ref[0] + yb_ref[0] + d_ref[...] * u_ref[0]
    a = jax.nn.gelu(y).astype(BF16)
    val = jnp.dot(a, w_ref[:, 0:D_MODEL], preferred_element_type=F32)
    gate = jnp.dot(a, w_ref[:, D_MODEL:2 * D_MODEL], preferred_element_type=F32)
    o_ref[0] = (val * jax.nn.sigmoid(gate)).astype(o_ref.dtype)


def _s5_out(y_f, y_b, u, d_skip, w_glu, *, tm):
    _, rows, d = u.shape
    row_spec = pl.BlockSpec((1, tm, d), lambda i: (0, i, 0))
    return pl.pallas_call(
        _s5_out_kernel,
        grid=(rows // tm,),
        in_specs=[row_spec, row_spec, row_spec,
                  pl.BlockSpec((1, d), lambda i: (0, 0)),
                  pl.BlockSpec((d, 2 * d), lambda i: (0, 0))],
        out_specs=row_spec,
        out_shape=jax.ShapeDtypeStruct((1, rows, d), BF16),
        compiler_params=_params("parallel"),
        name="s5_out",
    )(y_f, y_b, u, d_skip.astype(F32).reshape(1, d), w_glu.astype(BF16))


def _s5_mixer(u_lat, u_ctx, nb, lam_re, lam_im, log_dt, b_re, b_im, c_re, c_im, d_skip, w_glu):
    lbr, lbi, bbr, bbi = _s5_discretise(lam_re, lam_im, log_dt, b_re, b_im)
    eye = jnp.eye(S5_TILE_G, dtype=F32)
    bb = jnp.stack([bbr, bbi], axis=1).reshape(2, 2, S5_GROUP, S5_NT, S5_TILE_G, S5_STATE)
    wb = jnp.einsum('dzcktp,ts->dktczsp', bb, eye).reshape(2, S5_NT, LANES, 2 * S5_SL).astype(BF16)
    cc = jnp.stack([c_re.astype(F32), -c_im.astype(F32)], axis=0).reshape(2, S5_NT, S5_TILE_G, S5_GROUP, S5_STATE)
    wc = jnp.einsum('zktcp,ts->kzsptc', cc, eye).reshape(S5_NT, 2 * S5_SL, LANES).astype(BF16)
    lam = jnp.stack([lbr, lbi], axis=1).reshape(2, 2, S5_NT, 1, S5_SL).transpose(0, 2, 1, 3, 4)
    x0 = jnp.zeros((S5_NT, 2, nb, S5_SL), F32)
    yc_f, x_f = _s5_scan(u_ctx, wb[0], lam[0], wc, x0, reverse=False, nb=nb)
    yc_b, x_b = _s5_scan(u_ctx, wb[1], lam[1], wc, x0, reverse=True, nb=nb)
    yl_f, _ = _s5_scan(u_lat, wb[0], lam[0], wc, x_f, reverse=False, nb=nb)
    yl_b, _ = _s5_scan(u_lat, wb[1], lam[1], wc, x_b, reverse=True, nb=nb)
    y_lat = _s5_out(yl_f, yl_b, u_lat, d_skip, w_glu, tm=512)
    y_ctx = _s5_out(yc_f, yc_b, u_ctx, d_skip, w_glu, tm=512)
    return y_lat, y_ctx


_NT_DIMS = (((1,), (1,)), ((), ()))
_TN_DIMS = (((0,), (0,)), ((), ()))


def _tri_mask(n, reverse):
    r = lax.broadcasted_iota(jnp.int32, (n, n), 0)
    c = lax.broadcasted_iota(jnp.int32, (n, n), 1)
    return (c >= r) if reverse else (c <= r)


def _cumsum_scan(g, reverse):
    n = g.shape[0]
    rows = lax.broadcasted_iota(jnp.int32, (n, 1), 0)
    d = 1
    while d < n:
        if reverse:
            shifted = jnp.where(rows < n - d, pltpu.roll(g, n - d, axis=0), 0.0)
        else:
            shifted = jnp.where(rows >= d, pltpu.roll(g, d, axis=0), 0.0)
        g = g + shifted
        d *= 2
    return g


def _cumsum_f32(mask_bf16, g):
    g1 = g.astype(BF16)
    r1 = g - g1.astype(F32)
    g2 = r1.astype(BF16)
    g3 = (r1 - g2.astype(F32)).astype(BF16)
    dot = lambda t: jnp.dot(mask_bf16, t, preferred_element_type=F32)
    return dot(g1) + dot(g2) + dot(g3)


def _gla_scan_kernel(q_ref, k_ref, v_ref, glr_ref, wg_ref, bg_ref, s0_ref, o_ref, s_ref, *, reverse):
    @pl.when(pl.program_id(1) == 0)
    def _():
        s_ref[...] = s0_ref[...]

    nb, ch = q_ref.shape[0], q_ref.shape[1]
    mask = _tri_mask(ch, reverse)
    end = 0 if reverse else ch - 1
    glr = glr_ref[...].reshape(nb * ch, LANES).astype(BF16)
    pre_all = jnp.dot(glr, wg_ref[...], preferred_element_type=F32) + bg_ref[...]
    pairs = [(r, h) for r in range(nb) for h in range(GLA_HEADS)]
    sk = lambda h: slice(h * GLA_DK, (h + 1) * GLA_DK)
    sv = lambda h: slice(h * GLA_DV, (h + 1) * GLA_DV)
    q_dec, k_dec, k_end, decay_end = [], [], [], []
    for r in range(nb):
        g = jax.nn.log_sigmoid(pre_all[r * ch:(r + 1) * ch]) / GLA_GATE_NORM
        b = _cumsum_scan(g, reverse)
        b_end = b[end:end + 1, :]
        k = k_ref[r].astype(F32)
        q_dec.append((q_ref[r].astype(F32) * GLA_DK ** -0.5 * jnp.exp(b)).astype(BF16))
        k_dec.append((k * jnp.exp(-b)).astype(BF16))
        k_end.append((k * jnp.exp(b_end - b)).astype(BF16))
        decay_end.append(jnp.exp(b_end))
    scores = [lax.dot_general(q_dec[r][:, sk(h)], k_dec[r][:, sk(h)], _NT_DIMS, preferred_element_type=F32)
              for r, h in pairs]
    scores = [jnp.where(mask, s, 0.0).astype(BF16) for s in scores]
    states = [s_ref[r, h] for r, h in pairs]
    inter = [lax.dot_general(q_dec[r][:, sk(h)], st.astype(BF16), _NT_DIMS, preferred_element_type=F32)
             for (r, h), st in zip(pairs, states)]
    intra = [jnp.dot(s, v_ref[r, :, sv(h)], preferred_element_type=F32) for (r, h), s in zip(pairs, scores)]
    upd = [lax.dot_general(v_ref[r, :, sv(h)], k_end[r][:, sk(h)], _TN_DIMS, preferred_element_type=F32)
           for r, h in pairs]
    for i, (r, h) in enumerate(pairs):
        o_ref[r, :, sv(h)] = (intra[i] + inter[i]).astype(o_ref.dtype)
        s_ref[r, h] = decay_end[r][:, sk(h)] * states[i] + upd[i]


def _gla_scan(z, glr, wg, bg, s0, *, reverse, ch):
    bsz, length, _ = z.shape
    nc = length // ch
    nb = GLA_ROWS if bsz % GLA_ROWS == 0 else 1
    cix = (lambda i: nc - 1 - i) if reverse else (lambda i: i)
    st_spec = pl.BlockSpec((nb, GLA_HEADS, GLA_DV, GLA_DK), lambda b, i: (b, 0, 0, 0))
    shp = jax.ShapeDtypeStruct
    return pl.pallas_call(
        functools.partial(_gla_scan_kernel, reverse=reverse),
        grid=(bsz // nb, nc),
        in_specs=[pl.BlockSpec((nb, ch, GLA_QK), lambda b, i: (b, cix(i), 0)),
                  pl.BlockSpec((nb, ch, GLA_QK), lambda b, i: (b, cix(i), 1)),
                  pl.BlockSpec((nb, ch, GLA_VD), lambda b, i: (b, cix(i), 1)),
                  pl.BlockSpec((nb, ch, LANES), lambda b, i: (b, cix(i), 0)),
                  pl.BlockSpec((LANES, GLA_QK), lambda b, i: (0, 0)),
                  pl.BlockSpec((1, GLA_QK), lambda b, i: (0, 0)),
                  st_spec],
        out_specs=[pl.BlockSpec((nb, ch, GLA_VD), lambda b, i: (b, cix(i), 0)), st_spec],
        out_shape=[shp((bsz, length, GLA_VD), BF16), shp(s0.shape, F32)],
        compiler_params=_params("parallel", "arbitrary"),
        name="gla_scan",
    )(z, z, z, glr, wg, bg, s0)


def _gla_out_kernel(of_ref, ob_ref, r_ref, ng_ref, o_ref):
    for h in range(GLA_HEADS):
        sv = slice(h * GLA_DV, (h + 1) * GLA_DV)
        o = of_ref[0, :, sv].astype(F32) + ob_ref[0, :, sv].astype(F32)
        o = _rms(o, ng_ref[...])
        o_ref[0, :, sv] = (o * _silu(r_ref[0, :, sv].astype(F32))).astype(o_ref.dtype)


def _gla_out(o_f, o_b, z, norm_g, *, tm):
    bsz, length, _ = o_f.shape
    seq = pl.BlockSpec((1, tm, GLA_VD), lambda b, i: (b, i, 0))
    return pl.pallas_call(
        _gla_out_kernel,
        grid=(bsz, length // tm),
        in_specs=[seq, seq, pl.BlockSpec((1, tm, D_MODEL), lambda b, i: (b, i, 2)),
                  pl.BlockSpec((1, GLA_DV), lambda b, i: (0, 0))],
        out_specs=seq,
        out_shape=jax.ShapeDtypeStruct((bsz, length, GLA_VD), BF16),
        compiler_params=_params("parallel", "parallel"),
        name="gla_out",
    )(o_f, o_b, z, norm_g.astype(F32).reshape(1, GLA_DV))


def _gla_weights(w_in, w_gate, b_gate):
    n_main = 2 * GLA_QK + GLA_VD + D_MODEL
    w_main = w_in[:, :n_main].astype(BF16)
    w_glr = jnp.pad(w_in[:, n_main:], ((0, 0), (0, LANES - 2 * GLA_RANK))).astype(BF16)
    wg = jnp.zeros((2, LANES, GLA_QK), F32)
    wg = wg.at[0, :GLA_RANK].set(w_gate[0].astype(F32)).at[1, GLA_RANK:2 * GLA_RANK].set(w_gate[1].astype(F32))
    return w_main, w_glr, wg.astype(BF16), b_gate.astype(F32).reshape(2, 1, GLA_QK)


def _gla_mixer(z_lat, glr_lat, z_ctx, glr_ctx, wg, bg, norm_g):
    bsz = z_lat.shape[0]
    s0 = jnp.zeros((bsz, GLA_HEADS, GLA_DV, GLA_DK), F32)

    def bidirectional(z, glr, s_f, s_b, tm):
        o_f, s_f = _gla_scan(z, glr, wg[0], bg[0], s_f, reverse=False, ch=GLA_CHUNK)
        o_b, s_b = _gla_scan(z, glr, wg[1], bg[1], s_b, reverse=True, ch=GLA_CHUNK)
        return _gla_out(o_f, o_b, z, norm_g, tm=tm), s_f, s_b

    t_ctx, s_f, s_b = bidirectional(z_ctx, glr_ctx, s0, s0, z_ctx.shape[1])
    t_lat, _, _ = bidirectional(z_lat, glr_lat, s_f, s_b, 512)
    return t_lat, t_ctx


def _dot3(a, b):
    a1 = a.astype(BF16)
    a2 = (a - a1.astype(F32)).astype(BF16)
    b1 = b.astype(BF16)
    b2 = (b - b1.astype(F32)).astype(BF16)
    dot = lambda s, t: jnp.dot(s, t, preferred_element_type=F32)
    return dot(a1, b1) + dot(a1, b2) + dot(a2, b1)


def _hy_filter_kernel(bands_ref, w10_ref, w1c_ref, w1s_ref, b1_ref, w2_ref, b2_ref, w3_ref, fr_ref, dl_ref,
                      ha_ref, hb_ref, *, length):
    tm = ha_ref.shape[0]
    n = (pl.program_id(0) * tm + lax.broadcasted_iota(jnp.int32, (tm, 1), 0)).astype(F32)

    def mlp(pos):
        t = pos * (1.0 / length)
        ang = (2.0 * math.pi * t) * bands_ref[...]
        pre = (t * w10_ref[...] + _dot3(jnp.cos(ang), w1c_ref[...]) - _dot3(jnp.sin(ang), w1s_ref[...])
               + b1_ref[...])
        z = jnp.sin(fr_ref[...] * pre)
        z = jnp.sin(fr_ref[...] * (_dot3(z, w2_ref[...]) + b2_ref[...]))
        return z, jnp.exp(-t * dl_ref[...])

    z, dec = mlp(n)
    h_f = _dot3(z, w3_ref[:, 0:D_MODEL]) * dec
    h_b0 = _dot3(z, w3_ref[:, D_MODEL:2 * D_MODEL]) * dec
    ha_ref[...] = (h_f + jnp.where(n == 0.0, h_b0, 0.0)).astype(ha_ref.dtype)
    zb, decb = mlp(length - n)
    h_b = _dot3(zb, w3_ref[:, D_MODEL:2 * D_MODEL]) * decb
    hb_ref[...] = jnp.where(n > 0.0, h_b, 0.0).astype(hb_ref.dtype)


def _hy_filter(length, w1, b1, w2, b2, w3, freq):
    f32 = lambda t: t.astype(F32)
    nb = HY_BANDS
    bands = jnp.pad(jnp.linspace(1e-4, nb - 1, nb, dtype=F32), (0, LANES - nb)).reshape(1, LANES)
    w1 = f32(w1)
    w1c = jnp.pad(w1[1:1 + nb], ((0, LANES - nb), (0, 0)))
    w1s = jnp.pad(w1[1 + nb:1 + 2 * nb], ((0, LANES - nb), (0, 0)))
    log_target = math.log(HY_DECAY_TARGET)
    deltas = jnp.abs(jnp.linspace(log_target / HY_SLOW_DECAY, log_target / HY_FAST_DECAY, D_MODEL, dtype=F32))
    tm = min(length, 512)
    args = [bands, w1[0:1], w1c, w1s, f32(b1).reshape(1, -1), f32(w2), f32(b2).reshape(1, -1), f32(w3),
            f32(freq).reshape(1, -1), deltas.reshape(1, D_MODEL)]
    out_spec = pl.BlockSpec((tm, D_MODEL), lambda i: (i, 0))
    return pl.pallas_call(
        functools.partial(_hy_filter_kernel, length=length),
        grid=(length // tm,),
        in_specs=[pl.BlockSpec(a.shape, lambda i: (0, 0)) for a in args],
        out_specs=[out_spec, out_spec],
        out_shape=[jax.ShapeDtypeStruct((length, D_MODEL), BF16)] * 2,
        compiler_params=_params("parallel"),
        name="hy_filter",
    )(*args)


def _conv3(x_ref, xp_ref, xn_ref, cw_ref, cb_ref, r, nr):
    tm = x_ref.shape[1]
    x = x_ref[0].astype(F32)
    prev_row = jnp.where(r > 0, xp_ref[0][HALO - 1:HALO, :].astype(F32), 0.0)
    next_row = jnp.where(r < nr - 1, xn_ref[0][0:1, :].astype(F32), 0.0)
    rows = lax.broadcasted_iota(jnp.int32, (tm, 1), 0)
    x_prev = jnp.where(rows == 0, prev_row, pltpu.roll(x, 1, axis=0))
    x_next = jnp.where(rows == tm - 1, next_row, pltpu.roll(x, tm - 1, axis=0))
    return cw_ref[0:1, :] * x_prev + cw_ref[1:2, :] * x + cw_ref[2:3, :] * x_next + cb_ref[...]


def _conv3_specs(tm, length, ct, lane_block):
    hb, nhb = tm // HALO, length // HALO
    return [pl.BlockSpec((1, tm, ct), lambda b, r, j: (b, r, lane_block(j))),
            pl.BlockSpec((1, HALO, ct), lambda b, r, j: (b, jnp.maximum(r * hb - 1, 0), lane_block(j))),
            pl.BlockSpec((1, HALO, ct), lambda b, r, j: (b, jnp.minimum((r + 1) * hb, nhb - 1), lane_block(j))),
            pl.BlockSpec((3, ct), lambda b, r, j: (0, lane_block(j))),
            pl.BlockSpec((1, ct), lambda b, r, j: (0, lane_block(j)))]


def _hy_conv_kernel(*refs):
    r, nr = pl.program_id(1), pl.num_programs(1)
    u = [_conv3(*refs[5 * a:5 * a + 5], r, nr) for a in range(3)]
    x0_ref, p_ref = refs[15:]
    x0_ref[0] = u[0].astype(x0_ref.dtype)
    p_ref[0] = (u[2] * u[1]).astype(p_ref.dtype)


def _hy_conv(z, conv_w, conv_b, *, tm):
    bsz, length, _ = z.shape
    nj = D_MODEL // HY_CT
    cw, cb = conv_w.astype(F32), conv_b.astype(F32).reshape(1, -1)
    in_specs, args = [], []
    for a in range(3):
        in_specs += _conv3_specs(tm, length, HY_CT, lambda j, a=a: a * nj + j)
        args += [z, z, z, cw, cb]
    out_spec = pl.BlockSpec((1, tm, HY_CT), lambda b, r, j: (b, r, j))
    return pl.pallas_call(
        _hy_conv_kernel,
        grid=(bsz, length // tm, nj),
        in_specs=in_specs,
        out_specs=[out_spec, out_spec],
        out_shape=[jax.ShapeDtypeStruct((bsz, length, D_MODEL), BF16)] * 2,
        compiler_params=_params("parallel", "parallel", "parallel"),
        name="hy_conv",
    )(*args)


def _dft_cs(rows, cols, modulus):
    m = (rows.astype(jnp.int32)[:, None] * cols.astype(jnp.int32)[None, :]) % modulus
    theta = m.astype(F32) * (2.0 * math.pi / modulus)
    return jnp.cos(theta), jnp.sin(theta)


def _cplx_rows(c, s):
    return jnp.concatenate([jnp.concatenate([c, s], axis=1), jnp.concatenate([-s, c], axis=1)], axis=0)


def _hy_s1_kernel(*refs, real):
    if real:
        zr_ref, f_ref, o_ref = refs
        z = zr_ref[0]
    else:
        zr_ref, zi_ref, f_ref, o_ref = refs
        z = jnp.concatenate([zr_ref[0], zi_ref[0]], axis=0)
    res = jnp.dot(f_ref[...], z, preferred_element_type=F32)
    o_ref[0, 0] = res[:HY_N1].astype(o_ref.dtype)
    o_ref[0, 1] = res[HY_N1:].astype(o_ref.dtype)


def _hy_stage1(p, f, *, real):
    nseq, length, d = p.shape
    n1h = HY_N1 // 2
    cols = (length // n1h) * d
    pv = p.reshape(nseq, n1h, cols)
    npair = nseq if real else nseq // 2
    tc = HY_TC
    in_specs = [pl.BlockSpec((1, n1h, tc), lambda q, j: (q, 0, j))]
    args = [pv]
    if not real:
        in_specs.append(pl.BlockSpec((1, n1h, tc), lambda q, j: (q + npair, 0, j)))
        args.append(pv)
    in_specs.append(pl.BlockSpec(f.shape, lambda q, j: (0, 0)))
    return pl.pallas_call(
        functools.partial(_hy_s1_kernel, real=real),
        grid=(npair, cols // tc),
        in_specs=in_specs,
        out_specs=pl.BlockSpec((1, 2, HY_N1, tc), lambda q, j: (q, 0, 0, j)),
        out_shape=jax.ShapeDtypeStruct((npair, 2, HY_N1, cols), BF16),
        compiler_params=_params("parallel", "parallel"),
        name="hy_stage1",
    )(*args, f)


def _hy_s2f_kernel(a_ref, g_ref, o_ref):
    a = jnp.concatenate([a_ref[0, 0, 0], a_ref[0, 1, 0]], axis=0)
    o_ref[0, 0] = jnp.dot(g_ref[0], a, preferred_element_type=F32)


def _hy_stage2_filter(a, g):
    nseq = a.shape[0]
    av = a.reshape(nseq, 2, HY_N1, HY_N2, D_MODEL)
    return pl.pallas_call(
        _hy_s2f_kernel,
        grid=(HY_N1, nseq),
        in_specs=[pl.BlockSpec((1, 2, 1, HY_N2, D_MODEL), lambda k, q: (q, 0, k, 0, 0)),
                  pl.BlockSpec((1, 2 * HY_N2, 2 * HY_N2), lambda k, q: (k, 0, 0))],
        out_specs=pl.BlockSpec((1, 1, 2 * HY_N2, D_MODEL), lambda k, q: (q, k, 0, 0)),
        out_shape=jax.ShapeDtypeStruct((nseq, HY_N1, 2 * HY_N2, D_MODEL), F32),
        compiler_params=_params("parallel", "parallel"),
        name="hy_stage2_filter",
    )(av, g)


def _cmul(x, h, n):
    xr, xi, hr, hi = x[:n], x[n:], h[:n], h[n:]
    return jnp.concatenate([xr * hr - xi * hi, xr * hi + xi * hr], axis=0)


def _hy_s2_kernel(a_ref, g_ref, gi_ref, h_ref, o_ref):
    sign = (1 - 2 * (pl.program_id(0) % 2)).astype(F32)
    h = h_ref[0, 0] + sign * h_ref[1, 0]
    pairs = range(a_ref.shape[0])
    x = [jnp.dot(g_ref[0], jnp.concatenate([a_ref[q, 0, 0], a_ref[q, 1, 0]], axis=0),
                 preferred_element_type=F32) for q in pairs]
    y = [_cmul(x[q], h, HY_N2).astype(BF16) for q in pairs]
    b = [jnp.dot(gi_ref[0], y[q], preferred_element_type=F32) for q in pairs]
    for q in pairs:
        o_ref[q, 0, 0] = b[q][:HY_N2].astype(o_ref.dtype)
        o_ref[q, 1, 0] = b[q][HY_N2:].astype(o_ref.dtype)


def _hy_stage2(a, g, gi, hspec):
    npair = a.shape[0]
    av = a.reshape(npair, 2, HY_N1, HY_N2, D_MODEL)
    blk = pl.BlockSpec((npair, 2, 1, HY_N2, D_MODEL), lambda k: (0, 0, k, 0, 0))
    mat = pl.BlockSpec((1, 2 * HY_N2, 2 * HY_N2), lambda k: (k, 0, 0))
    out = pl.pallas_call(
        _hy_s2_kernel,
        grid=(HY_N1,),
        in_specs=[blk, mat, mat,
                  pl.BlockSpec((2, 1, 2 * HY_N2, D_MODEL), lambda k: (0, k, 0, 0))],
        out_specs=blk,
        out_shape=jax.ShapeDtypeStruct(av.shape, BF16),
        compiler_params=_params("parallel"),
        name="hy_stage2",
    )(av, g, gi, hspec)
    return out.reshape(a.shape)


def _hy_s1inv_kernel(b_ref, f_ref, x0_ref, p_ref, bias_ref, o_ref):
    bc = jnp.concatenate([b_ref[0, 0], b_ref[0, 1]], axis=0)
    y = jnp.dot(f_ref[...], bc, preferred_element_type=F32)
    n1h = HY_N1 // 2
    for half in range(2):
        conv = y[half * n1h:(half + 1) * n1h]
        u = p_ref[half, 0].astype(F32)
        o_ref[half, 0] = (x0_ref[half, 0].astype(F32) * (conv + u * bias_ref[...])).astype(o_ref.dtype)


def _hy_stage1_inv(b, f, x0, p, bias):
    npair, _, _, cols = b.shape
    bsz, length, d = x0.shape
    n1h = HY_N1 // 2
    tc = HY_TC
    view = lambda t: t.reshape(2, npair, n1h, cols)
    seq = pl.BlockSpec((2, 1, n1h, tc), lambda q, j: (0, q, 0, j))
    out = pl.pallas_call(
        _hy_s1inv_kernel,
        grid=(npair, cols // tc),
        in_specs=[pl.BlockSpec((1, 2, HY_N1, tc), lambda q, j: (q, 0, 0, j)),
                  pl.BlockSpec(f.shape, lambda q, j: (0, 0)),
                  seq, seq,
                  pl.BlockSpec((1, tc), lambda q, j: (0, 0))],
        out_specs=seq,
        out_shape=jax.ShapeDtypeStruct((2, npair, n1h, cols), BF16),
        compiler_params=_params("parallel", "parallel"),
        name="hy_stage1_inv",
    )(b, f, view(x0), view(p), jnp.tile(bias.astype(F32).reshape(1, d), (1, tc // d)))
    return out.reshape(bsz, length, d)


def _hy_ctx_filter_kernel(ha_ref, hb_ref, f_ref, o_ref):
    h2 = jnp.concatenate([ha_ref[...], hb_ref[...]], axis=0)
    o_ref[...] = jnp.dot(f_ref[...], h2, preferred_element_type=F32)


def _hy_ctx_kernel(zr_ref, zi_ref, f_ref, fi_ref, h_ref, x0_ref, p_ref, bias_ref, o_ref):
    length = zr_ref.shape[1]
    z = jnp.concatenate([zr_ref[0], zi_ref[0]], axis=0)
    x = jnp.dot(f_ref[...], z, preferred_element_type=F32)
    y = _cmul(x, h_ref[...], 2 * length).astype(BF16)
    out = jnp.dot(fi_ref[...], y, preferred_element_type=F32)
    for half in range(2):
        conv = out[half * length:(half + 1) * length]
        u = p_ref[half, 0].astype(F32)
        o_ref[half, 0] = (x0_ref[half, 0].astype(F32) * (conv + u * bias_ref[...])).astype(o_ref.dtype)


def _hy_ctx_conv(x0, p, ha, hb, bias):
    bsz, length, d = p.shape
    npair = bsz // 2
    n = 2 * length
    k = jnp.arange(n)
    c, s = _dft_cs(k, jnp.arange(length), n)
    f = _cplx_rows(c, s).astype(BF16)
    fi = (_cplx_rows(c.T, -s.T) * (1.0 / n)).astype(BF16)
    cf, sf = _dft_cs(k, k, n)
    ffilt = jnp.concatenate([cf, -sf], axis=0).astype(BF16)
    full = lambda t: pl.BlockSpec(t.shape, lambda *_: (0,) * t.ndim)
    hspec = pl.pallas_call(
        _hy_ctx_filter_kernel,
        out_shape=jax.ShapeDtypeStruct((2 * n, d), F32),
        compiler_params=_params(),
        name="hy_ctx_filter",
    )(ha, hb, ffilt)
    view = lambda t: t.reshape(2, npair, length, d)
    seq = pl.BlockSpec((2, 1, length, d), lambda q: (0, q, 0, 0))
    bias2 = bias.astype(F32).reshape(1, d)
    out = pl.pallas_call(
        _hy_ctx_kernel,
        grid=(npair,),
        in_specs=[pl.BlockSpec((1, length, d), lambda q: (q, 0, 0)),
                  pl.BlockSpec((1, length, d), lambda q: (q + npair, 0, 0)),
                  full(f), full(fi), full(hspec), seq, seq, full(bias2)],
        out_specs=seq,
        out_shape=jax.ShapeDtypeStruct((2, npair, length, d), BF16),
        compiler_params=_params("parallel"),
        name="hy_ctx_conv",
    )(p, p, f, fi, hspec, view(x0), view(p), bias2)
    return out.reshape(bsz, length, d)


def _hy_lat_conv(x0, p, ha, hb, bias):
    bsz, length, d = p.shape
    assert 2 * length == HY_N1 * HY_N2 and d == D_MODEL and bsz % 2 == 0
    n = 2 * length
    k1, n1 = jnp.arange(HY_N1), jnp.arange(HY_N1 // 2)
    c1, s1 = _dft_cs(k1, n1, HY_N1)
    f1 = _cplx_rows(c1, s1).astype(BF16)
    f1_real = jnp.concatenate([c1, -s1], axis=0).astype(BF16)
    f1_inv = (_cplx_rows(c1.T, -s1.T) * (1.0 / n)).astype(BF16)
    k = (k1[:, None] + HY_N1 * jnp.arange(HY_N2)[None, :]).reshape(-1)
    c2, s2 = _dft_cs(k, jnp.arange(HY_N2), n)
    c2, s2 = c2.reshape(HY_N1, HY_N2, HY_N2), s2.reshape(HY_N1, HY_N2, HY_N2)
    g = jax.vmap(_cplx_rows)(c2, s2).astype(BF16)
    gi = jax.vmap(_cplx_rows)(c2.transpose(0, 2, 1), -s2.transpose(0, 2, 1)).astype(BF16)
    hfilt = jnp.stack([ha, hb], axis=0)
    hspec = _hy_stage2_filter(_hy_stage1(hfilt, f1_real, real=True), g)
    a = _hy_stage1(p, f1, real=False)
    b = _hy_stage2(a, g, gi, hspec)
    return _hy_stage1_inv(b, f1_inv, x0, p, bias)


def _hy_mixer(z_lat, z_ctx, conv_w, conv_b, f_w1, f_b1, f_w2, f_b2, f_w3, f_freq, f_bias):
    def run(z, tm, conv_fn):
        length = z.shape[1]
        x0, p = _hy_conv(z, conv_w, conv_b, tm=tm)
        ha, hb = _hy_filter(length, f_w1, f_b1, f_w2, f_b2, f_w3, f_freq)
        return conv_fn(x0, p, ha, hb, f_bias)

    return run(z_lat, 512, _hy_lat_conv), run(z_ctx, z_ctx.shape[1], _hy_ctx_conv)


def _ml_qkv_kernel(xm_ref, xp_ref, xn_ref, cw_ref, cb_ref, wq_ref, wk_ref, wv_ref, wg_ref, bg_ref,
                   q_ref, k_ref, v_ref, xc_ref, g_ref):
    j = pl.program_id(2)
    xm = xm_ref[0]
    conv = _conv3(xm_ref, xp_ref, xn_ref, cw_ref, cb_ref, pl.program_id(1), pl.num_programs(1))
    xc = _silu(conv).astype(BF16)
    subs = [slice(s * ML_BD, (s + 1) * ML_BD) for s in range(ML_CT // ML_BD)]
    bd = lambda x, w_ref: jnp.concatenate(
        [jnp.dot(x[:, sl], w_ref[s], preferred_element_type=F32) for s, sl in enumerate(subs)], axis=1)
    q, k, v = bd(xc, wq_ref), bd(xc, wk_ref), bd(xm, wv_ref)
    q, k, v = q.astype(BF16), k.astype(BF16), v.astype(BF16)
    q_ref[0], k_ref[0], v_ref[0], xc_ref[0] = q, k, v, xc
    contrib = (jnp.dot(q, wg_ref[0], preferred_element_type=F32)
               + jnp.dot(k, wg_ref[1], preferred_element_type=F32)
               + jnp.dot(v, wg_ref[2], preferred_element_type=F32))

    @pl.when(j == 0)
    def _():
        g_ref[0] = contrib

    @pl.when(j > 0)
    def _():
        g_ref[0] += contrib

    @pl.when(j == pl.num_programs(2) - 1)
    def _():
        g = g_ref[0] + bg_ref[...]
        lane = lax.broadcasted_iota(jnp.int32, g.shape, 1)
        g_ref[0] = jnp.where(lane % (2 * ML_HEADS) >= ML_HEADS, jax.nn.log_sigmoid(g), g)


def _ml_qkv(z, conv_w, conv_b, wq, wk, wv, wg, bg, *, tm):
    bsz, length, _ = z.shape
    nj = ML_INNER // ML_CT
    seq = lambda: pl.BlockSpec((1, tm, ML_CT), lambda b, r, j: (b, r, j))
    shp = jax.ShapeDtypeStruct
    return pl.pallas_call(
        _ml_qkv_kernel,
        grid=(bsz, length // tm, nj),
        in_specs=_conv3_specs(tm, length, ML_CT, lambda j: j) + [
                  pl.BlockSpec((ML_CT // ML_BD, ML_BD, ML_BD), lambda b, r, j: (j, 0, 0)),
                  pl.BlockSpec((ML_CT // ML_BD, ML_BD, ML_BD), lambda b, r, j: (j, 0, 0)),
                  pl.BlockSpec((ML_CT // ML_BD, ML_BD, ML_BD), lambda b, r, j: (j, 0, 0)),
                  pl.BlockSpec((3, ML_CT, LANES), lambda b, r, j: (0, j, 0)),
                  pl.BlockSpec((1, LANES), lambda b, r, j: (0, 0))],
        out_specs=[seq(), seq(), seq(), seq(),
                   pl.BlockSpec((1, tm, LANES), lambda b, r, j: (b, r, 0))],
        out_shape=[shp((bsz, length, ML_INNER), BF16)] * 4 + [shp((bsz, length, LANES), F32)],
        compiler_params=_params("parallel", "parallel", "arbitrary"),
        name="ml_qkv",
    )(z, z, z, conv_w, conv_b, wq, wk, wv, wg, bg)


def _ml_scan_kernel(q_ref, k_ref, v_ref, g_ref, c0_ref, n0_ref, m0_ref, h_ref, c_ref, n_ref, m_ref,
                    *, reverse, z):
    @pl.when(pl.program_id(1) == 0)
    def _():
        c_ref[...] = c0_ref[...]
        n_ref[...] = n0_ref[...]
        m_ref[...] = m0_ref[...]

    ch = q_ref.shape[1]
    mask = _tri_mask(ch, reverse)
    gates = g_ref[0]
    bcum = _cumsum_f32(mask.astype(BF16), gates)
    gates_t, bcum_t = gates.T, bcum.T
    end = 0 if reverse else ch - 1
    for h in range(ML_HEADS):
        li, lf = z * 2 * ML_HEADS + h, z * 2 * ML_HEADS + ML_HEADS + h
        sl = slice(h * ML_DH, (h + 1) * ML_DH)
        q, v = q_ref[0, :, sl], v_ref[0, :, sl]
        k = k_ref[0, :, sl].astype(F32) * ML_DH ** -0.5
        b_col, b_row = bcum[:, lf:lf + 1], bcum_t[lf:lf + 1, :]
        i_col, i_row = gates[:, li:li + 1], gates_t[li:li + 1, :]
        b_end = b_col[end:end + 1, :]
        m_prev = m_ref[0, h][0:1, 0:1]
        d_intra = jnp.where(mask, b_col - b_row + i_row, -jnp.inf)
        d_inter = b_col + m_prev
        m_tok = jnp.maximum(jnp.max(d_intra, axis=1, keepdims=True), d_inter)
        w_inter = jnp.exp(d_inter - m_tok)
        s = lax.dot_general(q, k.astype(BF16), _NT_DIMS, preferred_element_type=F32) * jnp.exp(d_intra - m_tok)
        c_mat = c_ref[0, h]
        n_vec = n_ref[0, h]
        num = (jnp.dot(s.astype(BF16), v, preferred_element_type=F32)
               + w_inter * jnp.dot(q, c_mat.astype(BF16), preferred_element_type=F32))
        den = (jnp.sum(s, axis=1, keepdims=True)
               + w_inter * jnp.sum(q.astype(F32) * n_vec, axis=1, keepdims=True))
        h_ref[0, :, sl] = (num / jnp.maximum(jnp.abs(den), jnp.exp(-m_tok))).astype(h_ref.dtype)
        d_state = b_end - b_col + i_col
        m_new = jnp.maximum(b_end + m_prev, jnp.max(d_state, axis=0, keepdims=True))
        w_prev = jnp.exp(b_end + m_prev - m_new)
        kw = k * jnp.exp(d_state - m_new)
        c_ref[0, h] = w_prev * c_mat + lax.dot_general(kw.astype(BF16), v, _TN_DIMS, preferred_element_type=F32)
        n_ref[0, h] = w_prev * n_vec + jnp.sum(kw, axis=0, keepdims=True)
        m_ref[0, h] = jnp.broadcast_to(m_new, m_ref.shape[2:])


def _ml_scan(q, k, v, gates, state, *, reverse, z, ch):
    bsz, length, _ = q.shape
    nc = length // ch
    cix = (lambda i: nc - 1 - i) if reverse else (lambda i: i)
    seq = lambda w: pl.BlockSpec((1, ch, w), lambda b, i: (b, cix(i), 0))
    st_specs = [pl.BlockSpec((1,) + s.shape[1:], lambda b, i: (b, 0, 0, 0)) for s in state]
    shp = jax.ShapeDtypeStruct
    out = pl.pallas_call(
        functools.partial(_ml_scan_kernel, reverse=reverse, z=z),
        grid=(bsz, nc),
        in_specs=[seq(ML_INNER), seq(ML_INNER), seq(ML_INNER), seq(LANES)] + st_specs,
        out_specs=[seq(ML_INNER)] + st_specs,
        out_shape=[shp((bsz, length, ML_INNER), BF16)] + [shp(s.shape, F32) for s in state],
        compiler_params=_params("parallel", "arbitrary"),
        name="ml_scan",
    )(q, k, v, gates, *state)
    return out[0], tuple(out[1:])


def _ml_out_kernel(hf_ref, hb_ref, xc_ref, z_ref, ng_ref, sk_ref, o_ref):
    for h in range(ML_HEADS):
        sl = slice(h * ML_DH, (h + 1) * ML_DH)
        o = hf_ref[0, :, sl].astype(F32) + hb_ref[0, :, sl].astype(F32)
        o = o - jnp.mean(o, axis=-1, keepdims=True)
        o = o * lax.rsqrt(jnp.mean(o * o, axis=-1, keepdims=True) + EPS) * ng_ref[:, sl]
        t = (o + sk_ref[:, sl] * xc_ref[0, :, sl].astype(F32)) * _silu(z_ref[0, :, sl].astype(F32))
        o_ref[0, :, sl] = t.astype(o_ref.dtype)


def _ml_out(h_f, h_b, xc, z, norm_g, skip, *, tm):
    bsz, length, _ = h_f.shape
    seq = pl.BlockSpec((1, tm, ML_INNER), lambda b, i: (b, i, 0))
    vec = pl.BlockSpec((1, ML_INNER), lambda b, i: (0, 0))
    return pl.pallas_call(
        _ml_out_kernel,
        grid=(bsz, length // tm),
        in_specs=[seq, seq, seq, pl.BlockSpec((1, tm, ML_INNER), lambda b, i: (b, i, 1)), vec, vec],
        out_specs=seq,
        out_shape=jax.ShapeDtypeStruct((bsz, length, ML_INNER), BF16),
        compiler_params=_params("parallel", "parallel"),
        name="ml_out",
    )(h_f, h_b, xc, z, norm_g.astype(F32).reshape(1, ML_INNER), skip.astype(F32).reshape(1, ML_INNER))


def _ml_weights(w_q, w_k, w_v, w_gates, b_gates):
    nb = ML_BD // ML_BLOCK
    eye = jnp.eye(nb, dtype=F32)

    def dense(w):
        w = w.astype(F32).reshape(ML_INNER // ML_BD, nb, ML_BLOCK, ML_BLOCK)
        return jnp.einsum('tncd,nm->tncmd', w, eye).reshape(ML_INNER // ML_BD, ML_BD, ML_BD).astype(BF16)

    ng = 2 * 2 * ML_HEADS
    wg = w_gates.astype(F32).reshape(2, 3, ML_INNER, 2 * ML_HEADS).transpose(1, 2, 0, 3).reshape(3, ML_INNER, ng)
    wg = jnp.pad(wg, ((0, 0), (0, 0), (0, LANES - ng))).astype(BF16)
    bg = jnp.pad(b_gates.astype(F32).reshape(1, ng), ((0, 0), (0, LANES - ng)))
    return dense(w_q), dense(w_k), dense(w_v), wg, bg


def _ml_mixer(z_lat, z_ctx, conv_w, conv_b, w_q, w_k, w_v, w_gates, b_gates, norm_g, skip):
    wq, wk, wv, wg, bg = _ml_weights(w_q, w_k, w_v, w_gates, b_gates)
    cw, cb = conv_w.astype(F32), conv_b.astype(F32).reshape(1, ML_INNER)
    bsz = z_lat.shape[0]
    state = (jnp.zeros((bsz, ML_HEADS, ML_DH, ML_DH), F32),
             jnp.zeros((bsz, ML_HEADS, 1, ML_DH), F32),
             jnp.full((bsz, ML_HEADS, 8, LANES), -jnp.inf, F32))

    def bidirectional(z, st_f, st_b, tm):
        q, k, v, xc, gates = _ml_qkv(z, cw, cb, wq, wk, wv, wg, bg, tm=tm)
        h_f, st_f = _ml_scan(q, k, v, gates, st_f, reverse=False, z=0, ch=ML_CHUNK)
        h_b, st_b = _ml_scan(q, k, v, gates, st_b, reverse=True, z=1, ch=ML_CHUNK)
        return _ml_out(h_f, h_b, xc, z, norm_g, skip, tm=tm), st_f, st_b

    t_ctx, st_f, st_b = bidirectional(z_ctx, state, state, z_ctx.shape[1])
    t_lat, _, _ = bidirectional(z_lat, st_f, st_b, 512)
    return t_lat, t_ctx


def kernel(x, c, ctx, c_ctx, mod_w, mod_b, norm_g, ffn_w_in, ffn_w_out, gla_w_in, gla_w_gate, gla_b_gate, gla_norm_g, gla_w_out, hy_w_in, hy_conv_w, hy_conv_b, hy_f_w1, hy_f_b1, hy_f_w2, hy_f_b2, hy_f_w3, hy_f_freq, hy_f_bias, hy_w_out, ml_w_in, ml_conv_w, ml_conv_b, ml_w_q, ml_w_k, ml_w_v, ml_w_gates, ml_b_gates, ml_norm_g, ml_skip, ml_w_out, s5_lam_re, s5_lam_im, s5_log_dt, s5_b_re, s5_b_im, s5_c_re, s5_c_im, s5_d, s5_w_glu):
    bsz, seq, d = x.shape
    ctx_len = ctx.shape[1]
    depth = mod_w.shape[0]
    mods = _modulation(c, c_ctx, mod_w, mod_b)
    lat = x.astype(F32)
    cx = ctx.astype(F32)
    tm_lat, tm_ctx = 512, ctx_len
    lat_col_major = False
    for i in range(depth):
        kind, j = i % N_MIXERS, i // N_MIXERS
        last = i == depth - 1
        if (kind >= 2) != lat_col_major:
            lat = _from_col_major(lat) if lat_col_major else _to_col_major(lat)
            lat_col_major = not lat_col_major
        mod_l = mods[i, :bsz]
        mod_c = mods[i, bsz:bsz + 1]
        win, wout = _ffn_weights(ffn_w_in[i], ffn_w_out[i])
        wmix = None
        tmajor = False
        if kind == 3:
            tmajor = True
            u_lat = _norm_mod(lat, mod_l, norm_g[i, 0], tm=tm_lat, tmajor=True)
            u_ctx = _norm_mod(cx, mod_c, norm_g[i, 0], tm=tm_ctx, tmajor=True)
            y_lat, y_ctx = _s5_mixer(u_lat.reshape(1, seq * bsz, d), u_ctx.reshape(1, ctx_len * bsz, d), bsz,
                                     s5_lam_re[j], s5_lam_im[j], s5_log_dt[j], s5_b_re[j], s5_b_im[j],
                                     s5_c_re[j], s5_c_im[j], s5_d[j], s5_w_glu[j])
            y_lat = y_lat.reshape(1, seq, bsz * d)
            y_ctx = y_ctx.reshape(1, ctx_len, bsz * d)
        elif kind == 0:
            w_main, w_glr, wg, bg = _gla_weights(gla_w_in[j], gla_w_gate[j], gla_b_gate[j])
            z_lat, glr_lat = _norm_proj(lat, mod_l, norm_g[i, 0], [w_main, w_glr], [BF16, F32], tm=tm_lat)
            z_ctx, glr_ctx = _norm_proj(cx, mod_c, norm_g[i, 0], [w_main, w_glr], [BF16, F32], tm=tm_ctx)
            y_lat, y_ctx = _gla_mixer(z_lat, glr_lat, z_ctx, glr_ctx, wg, bg, gla_norm_g[j])
            wmix = gla_w_out[j].astype(BF16)
        elif kind == 2:
            w_in = ml_w_in[j].astype(BF16)
            (z_lat,) = _norm_proj(lat, mod_l, norm_g[i, 0], [w_in], [BF16], tm=tm_lat)
            (z_ctx,) = _norm_proj(cx, mod_c, norm_g[i, 0], [w_in], [BF16], tm=tm_ctx)
            y_lat, y_ctx = _ml_mixer(z_lat, z_ctx, ml_conv_w[j], ml_conv_b[j], ml_w_q[j], ml_w_k[j], ml_w_v[j],
                                     ml_w_gates[j], ml_b_gates[j], ml_norm_g[j], ml_skip[j])
            wmix = ml_w_out[j].astype(BF16)
        else:
            w_in = hy_w_in[j].astype(BF16)
            (z_lat,) = _norm_proj(lat, mod_l, norm_g[i, 0], [w_in], [BF16], tm=tm_lat)
            (z_ctx,) = _norm_proj(cx, mod_c, norm_g[i, 0], [w_in], [BF16], tm=tm_ctx)
            y_lat, y_ctx = _hy_mixer(z_lat, z_ctx, hy_conv_w[j], hy_conv_b[j], hy_f_w1[j], hy_f_b1[j],
                                     hy_f_w2[j], hy_f_b2[j], hy_f_w3[j], hy_f_freq[j], hy_f_bias[j])
            wmix = hy_w_out[j].astype(BF16)
        lat = _post_ffn(lat, y_lat, mod_l, norm_g[i], wmix, win, wout, tm=tm_lat, tmajor=tmajor)
        if not last:
            cx = _post_ffn(cx, y_ctx, mod_c, norm_g[i], wmix, win, wout, tm=tm_ctx, tmajor=tmajor)
    return _from_col_major(lat) if lat_col_major else lat
```

```python
import functools
import math

import jax
import jax.numpy as jnp
from jax import lax
from jax.experimental import pallas as pl
from jax.experimental.pallas import tpu as pltpu

F32 = jnp.float32
BF16 = jnp.bfloat16

D_MODEL = 1024
GRID_W = 64
N_MOD = 6
EPS = 1e-6
FFN_HIDDEN = 2816
FFN_CHUNK = 256
N_FFN_CHUNKS = FFN_HIDDEN // FFN_CHUNK
MOD_ROWS = 16
VMEM_LIMIT = 56 * 1024 * 1024
LANES = 128
N_MIXERS = 4
PROJ_TN = 512

GLA_HEADS = 4
GLA_QK = D_MODEL // 2
GLA_VD = D_MODEL
GLA_DK = GLA_QK // GLA_HEADS
GLA_DV = GLA_VD // GLA_HEADS
GLA_RANK = 16
GLA_GATE_NORM = 16.0
GLA_CHUNK = 64
GLA_ROWS = 8

HALO = 16
HY_BANDS = 16
HY_DECAY_TARGET = 1e-2
HY_FAST_DECAY = 0.3
HY_SLOW_DECAY = 1.5
HY_CT = 256
HY_N1, HY_N2 = 64, 128
HY_TC = 8192

ML_HEADS = 4
ML_INNER = 2 * D_MODEL
ML_DH = ML_INNER // ML_HEADS
ML_BLOCK = 4
ML_CHUNK = 256
ML_CT = 512
ML_BD = 256

S5_GROUP = 16
S5_GROUPS = D_MODEL // S5_GROUP
S5_STATE = 64
S5_TILE_G = LANES // S5_GROUP
S5_NT = S5_GROUPS // S5_TILE_G
S5_SL = S5_TILE_G * S5_STATE
S5_TB = 64
S5_KT = 4


def _rms(x, g):
    return x * lax.rsqrt(jnp.mean(x * x, axis=-1, keepdims=True) + EPS) * g


def _silu(x):
    return x * jax.nn.sigmoid(x)


def _params(*sem):
    return pltpu.CompilerParams(dimension_semantics=sem, vmem_limit_bytes=VMEM_LIMIT)


def _mod_kernel(cc_ref, w_ref, b_ref, o_ref):
    a = _silu(cc_ref[...])
    o_ref[0] = jnp.dot(a.astype(BF16), w_ref[0].astype(BF16), preferred_element_type=F32) + b_ref[0]


def _modulation(c, c_ctx, mod_w, mod_b):
    depth, d, n = mod_w.shape
    bsz = c.shape[0]
    cc = jnp.concatenate([c.astype(F32), c_ctx.astype(F32)[None],
                          jnp.zeros((MOD_ROWS - bsz - 1, d), F32)], axis=0)
    tn = 1536
    out = pl.pallas_call(
        _mod_kernel,
        grid=(depth, n // tn),
        in_specs=[pl.BlockSpec((MOD_ROWS, d), lambda i, j: (0, 0)),
                  pl.BlockSpec((1, d, tn), lambda i, j: (i, 0, j)),
                  pl.BlockSpec((1, 1, tn), lambda i, j: (i, 0, j))],
        out_specs=pl.BlockSpec((1, MOD_ROWS, tn), lambda i, j: (i, 0, j)),
        out_shape=jax.ShapeDtypeStruct((depth, MOD_ROWS, n), F32),
        compiler_params=_params("parallel", "parallel"),
        name="adaln_mod",
    )(cc, mod_w, mod_b.reshape(depth, 1, n))
    return out.reshape(depth, MOD_ROWS, N_MOD, d)


def _to_col_major(x):
    bsz, length, d = x.shape
    return x.reshape(bsz, length // GRID_W, GRID_W, d).transpose(0, 2, 1, 3).reshape(bsz, length, d)


def _from_col_major(x):
    bsz, length, d = x.shape
    return x.reshape(bsz, GRID_W, length // GRID_W, d).transpose(0, 2, 1, 3).reshape(bsz, length, d)


def _seq_spec(tmajor, tm, width):
    if tmajor:
        return pl.BlockSpec((1, tm, width), lambda b, i: (0, i, b))
    return pl.BlockSpec((1, tm, width), lambda b, i: (b, i, 0))


def _mod_spec(mod):
    if mod.shape[0] > 1:
        return pl.BlockSpec((1, N_MOD, D_MODEL), lambda b, i: (b, 0, 0))
    return pl.BlockSpec((1, N_MOD, D_MODEL), lambda b, i: (0, 0, 0))


def _norm_kernel(x_ref, m_ref, g_ref, o_ref):
    h = _rms(x_ref[0], g_ref[...]) * (1.0 + m_ref[0, 1:2, :]) + m_ref[0, 0:1, :]
    o_ref[0] = h.astype(o_ref.dtype)


def _norm_mod(x, mod, g, *, tm, tmajor=False, out_dtype=F32):
    bsz, length, d = x.shape
    out_shape = (1, length, bsz * d) if tmajor else (bsz, length, d)
    return pl.pallas_call(
        _norm_kernel,
        grid=(bsz, length // tm),
        in_specs=[_seq_spec(False, tm, d), _mod_spec(mod), pl.BlockSpec((1, d), lambda b, i: (0, 0))],
        out_specs=_seq_spec(tmajor, tm, d),
        out_shape=jax.ShapeDtypeStruct(out_shape, out_dtype),
        compiler_params=_params("parallel", "parallel"),
        name="norm_mod",
    )(x, mod, g.reshape(1, d))


def _proj_kernel(*refs, n_w):
    x_ref, m_ref, g_ref = refs[:3]
    w_refs, o_refs = refs[3:3 + n_w], refs[3 + n_w:]
    h = (_rms(x_ref[0], g_ref[...]) * (1.0 + m_ref[0, 1:2, :]) + m_ref[0, 0:1, :]).astype(BF16)
    for w_ref, o_ref in zip(w_refs, o_refs):
        n = w_ref.shape[1]
        tn = min(n, PROJ_TN)
        for n0 in range(0, n, tn):
            o_ref[0, :, n0:n0 + tn] = jnp.dot(h, w_ref[:, n0:n0 + tn],
                                              preferred_element_type=F32).astype(o_ref.dtype)


def _norm_proj(x, mod, g, ws, dtypes, *, tm):
    bsz, length, d = x.shape
    full = lambda shape: pl.BlockSpec(shape, lambda b, i: (0,) * len(shape))
    return pl.pallas_call(
        functools.partial(_proj_kernel, n_w=len(ws)),
        grid=(bsz, length // tm),
        in_specs=[_seq_spec(False, tm, d), _mod_spec(mod), full((1, d))] + [full(w.shape) for w in ws],
        out_specs=[_seq_spec(False, tm, w.shape[1]) for w in ws],
        out_shape=[jax.ShapeDtypeStruct((bsz, length, w.shape[1]), dt) for w, dt in zip(ws, dtypes)],
        compiler_params=_params("parallel", "parallel"),
        name="norm_proj",
    )(x, mod, g.reshape(1, d), *ws)


def _ffn_kernel(*refs, has_wmix):
    if has_wmix:
        x_ref, y_ref, m_ref, g_ref, wmix_ref, win_ref, wout_ref, o_ref = refs
        y = jnp.dot(y_ref[0], wmix_ref[...], preferred_element_type=F32)
    else:
        x_ref, y_ref, m_ref, g_ref, win_ref, wout_ref, o_ref = refs
        y = y_ref[0].astype(F32)
    lat = x_ref[0] + m_ref[0, 2:3, :] * _rms(y, g_ref[1:2, :])
    h = (_rms(lat, g_ref[2:3, :]) * (1.0 + m_ref[0, 4:5, :]) + m_ref[0, 3:4, :]).astype(BF16)

    def body(c, acc):
        gate = jnp.dot(h, win_ref[0, c], preferred_element_type=F32)
        up = jnp.dot(h, win_ref[1, c], preferred_element_type=F32)
        a = (_silu(gate) * up).astype(BF16)
        return acc + jnp.dot(a, wout_ref[c], preferred_element_type=F32)

    acc = lax.fori_loop(0, N_FFN_CHUNKS, body, jnp.zeros(lat.shape, F32), unroll=True)
    o_ref[0] = lat + m_ref[0, 5:6, :] * _rms(acc, g_ref[3:4, :])


def _post_ffn(x, y, mod, g4, wmix, win, wout, *, tm, tmajor=False):
    bsz, length, d = x.shape
    width = y.shape[-1] // bsz if tmajor else y.shape[-1]
    full = lambda shape: pl.BlockSpec(shape, lambda b, i: (0,) * len(shape))
    in_specs = [_seq_spec(False, tm, d), _seq_spec(tmajor, tm, width), _mod_spec(mod), full((4, d))]
    args = [x, y, mod, g4]
    if wmix is not None:
        in_specs.append(full(wmix.shape))
        args.append(wmix)
    in_specs += [full(win.shape), full(wout.shape)]
    args += [win, wout]
    return pl.pallas_call(
        functools.partial(_ffn_kernel, has_wmix=wmix is not None),
        grid=(bsz, length // tm),
        in_specs=in_specs,
        out_specs=_seq_spec(False, tm, d),
        out_shape=jax.ShapeDtypeStruct(x.shape, F32),
        compiler_params=_params("parallel", "parallel"),
        name="post_ffn",
    )(*args)


def _ffn_weights(w_in, w_out):
    d = w_in.shape[0]
    win = w_in.astype(BF16).reshape(d, 2, N_FFN_CHUNKS, FFN_CHUNK).transpose(1, 2, 0, 3)
    wout = w_out.astype(BF16).reshape(N_FFN_CHUNKS, FFN_CHUNK, d)
    return win, wout


def _s5_disc_kernel(lre_ref, lim_ref, ldt_ref, bre_ref, bim_ref, lbr_ref, lbi_ref, bbr_ref, bbi_ref):
    lre, lim = lre_ref[...], lim_ref[...]
    dt = jnp.exp(ldt_ref[...])
    mag = jnp.exp(lre * dt)
    lbr = mag * jnp.cos(lim * dt)
    lbi = mag * jnp.sin(lim * dt)
    lbr_ref[...] = lbr
    lbi_ref[...] = lbi
    nr = lbr - 1.0
    den = lre * lre + lim * lim
    cr = (nr * lre + lbi * lim) / den
    ci = (lbi * lre - nr * lim) / den
    for dr in range(2):
        bbr_ref[dr] = cr[dr:dr + 1] * bre_ref[...] - ci[dr:dr + 1] * bim_ref[...]
        bbi_ref[dr] = cr[dr:dr + 1] * bim_ref[...] + ci[dr:dr + 1] * bre_ref[...]


def _s5_discretise(lam_re, lam_im, log_dt, b_re, b_im):
    gp = S5_GROUPS * S5_STATE
    flat = lambda t: t.astype(F32).reshape(2, gp)
    ldt = jnp.repeat(log_dt.astype(F32), S5_STATE, axis=1)
    bt = lambda t: t.astype(F32).reshape(gp, S5_GROUP).T
    shp = jax.ShapeDtypeStruct
    return pl.pallas_call(
        _s5_disc_kernel,
        out_shape=(shp((2, gp), F32), shp((2, gp), F32),
                   shp((2, S5_GROUP, gp), F32), shp((2, S5_GROUP, gp), F32)),
        name="s5_discretise",
    )(flat(lam_re), flat(lam_im), ldt, bt(b_re), bt(b_im))


def _s5_scan_kernel(u_ref, wb_ref, lam_ref, wc_ref, x0_ref, y_ref, xt_ref, bu_scr, st_scr, *, reverse, nb):
    i = pl.program_id(1)

    @pl.when(i == 0)
    def _():
        st_scr[...] = x0_ref[...]

    for kt in range(S5_KT):
        u = u_ref[0, :, kt * LANES:(kt + 1) * LANES].astype(BF16)
        bu_scr[kt] = jnp.dot(u, wb_ref[kt], preferred_element_type=F32)
    for kt in range(S5_KT):
        lr = jnp.broadcast_to(lam_ref[kt, 0], (nb, S5_SL))
        li = jnp.broadcast_to(lam_ref[kt, 1], (nb, S5_SL))
        xr, xi = st_scr[kt, 0], st_scr[kt, 1]
        for s in range(S5_TB):
            r0 = ((S5_TB - 1 - s) if reverse else s) * nb
            xr, xi = (lr * xr - li * xi + bu_scr[kt, r0:r0 + nb, 0:S5_SL],
                      lr * xi + li * xr + bu_scr[kt, r0:r0 + nb, S5_SL:2 * S5_SL])
            bu_scr[kt, r0:r0 + nb, 0:S5_SL] = xr
            bu_scr[kt, r0:r0 + nb, S5_SL:2 * S5_SL] = xi
        st_scr[kt, 0] = xr
        st_scr[kt, 1] = xi
        y_ref[0, :, kt * LANES:(kt + 1) * LANES] = jnp.dot(bu_scr[kt].astype(BF16), wc_ref[kt],
                                                            preferred_element_type=F32)

    @pl.when(i == pl.num_programs(1) - 1)
    def _():
        xt_ref[...] = st_scr[...]


def _s5_scan(u, wb, lam, wc, x0, *, reverse, nb):
    _, rows, d = u.shape
    nt = rows // (S5_TB * nb)
    tix = (lambda i: nt - 1 - i) if reverse else (lambda i: i)
    blk = S5_TB * nb
    shp = jax.ShapeDtypeStruct
    return pl.pallas_call(
        functools.partial(_s5_scan_kernel, reverse=reverse, nb=nb),
        grid=(S5_NT // S5_KT, nt),
        in_specs=[pl.BlockSpec((1, blk, S5_KT * LANES), lambda k, i: (0, tix(i), k)),
                  pl.BlockSpec((S5_KT, LANES, 2 * S5_SL), lambda k, i: (k, 0, 0)),
                  pl.BlockSpec((S5_KT, 2, 1, S5_SL), lambda k, i: (k, 0, 0, 0)),
                  pl.BlockSpec((S5_KT, 2 * S5_SL, LANES), lambda k, i: (k, 0, 0)),
                  pl.BlockSpec((S5_KT, 2, nb, S5_SL), lambda k, i: (k, 0, 0, 0))],
        out_specs=[pl.BlockSpec((1, blk, S5_KT * LANES), lambda k, i: (0, tix(i), k)),
                   pl.BlockSpec((S5_KT, 2, nb, S5_SL), lambda k, i: (k, 0, 0, 0))],
        out_shape=(shp((1, rows, d), F32), shp((S5_NT, 2, nb, S5_SL), F32)),
        scratch_shapes=[pltpu.VMEM((S5_KT, blk, 2 * S5_SL), F32), pltpu.VMEM((S5_KT, 2, nb, S5_SL), F32)],
        compiler_params=_params("parallel", "arbitrary"),
        name="s5_scan",
    )(u, wb, lam, wc, x0)


def _s5_out_kernel(yf_ref, yb_ref, u_ref, d_ref, w_ref, o_ref):
    y = yf_ref[0] + yb_ref[0] + d_ref[...] * u_ref[0]
    a = jax.nn.gelu(y).astype(BF16)
    val = jnp.dot(a, w_ref[:, 0:D_MODEL], preferred_element_type=F32)
    gate = jnp.dot(a, w_ref[:, D_MODEL:2 * D_MODEL], preferred_element_type=F32)
    o_ref[0] = (val * jax.nn.sigmoid(gate)).astype(o_ref.dtype)


def _s5_out(y_f, y_b, u, d_skip, w_glu, *, tm):
    _, rows, d = u.shape
    row_spec = pl.BlockSpec((1, tm, d), lambda i: (0, i, 0))
    return pl.pallas_call(
        _s5_out_kernel,
        grid=(rows // tm,),
        in_specs=[row_spec, row_spec, row_spec,
                  pl.BlockSpec((1, d), lambda i: (0, 0)),
                  pl.BlockSpec((d, 2 * d), lambda i: (0, 0))],
        out_specs=row_spec,
        out_shape=jax.ShapeDtypeStruct((1, rows, d), BF16),
        compiler_params=_params("parallel"),
        name="s5_out",
    )(y_f, y_b, u, d_skip.astype(F32).reshape(1, d), w_glu.astype(BF16))


def _s5_mixer(u_lat, u_ctx, nb, lam_re, lam_im, log_dt, b_re, b_im, c_re, c_im, d_skip, w_glu):
    lbr, lbi, bbr, bbi = _s5_discretise(lam_re, lam_im, log_dt, b_re, b_im)
    eye = jnp.eye(S5_TILE_G, dtype=F32)
    bb = jnp.stack([bbr, bbi], axis=1).reshape(2, 2, S5_GROUP, S5_NT, S5_TILE_G, S5_STATE)
    wb = jnp.einsum('dzcktp,ts->dktczsp', bb, eye).reshape(2, S5_NT, LANES, 2 * S5_SL).astype(BF16)
    cc = jnp.stack([c_re.astype(F32), -c_im.astype(F32)], axis=0).reshape(2, S5_NT, S5_TILE_G, S5_GROUP, S5_STATE)
    wc = jnp.einsum('zktcp,ts->kzsptc', cc, eye).reshape(S5_NT, 2 * S5_SL, LANES).astype(BF16)
    lam = jnp.stack([lbr, lbi], axis=1).reshape(2, 2, S5_NT, 1, S5_SL).transpose(0, 2, 1, 3, 4)
    x0 = jnp.zeros((S5_NT, 2, nb, S5_SL), F32)
    yc_f, x_f = _s5_scan(u_ctx, wb[0], lam[0], wc, x0, reverse=False, nb=nb)
    yc_b, x_b = _s5_scan(u_ctx, wb[1], lam[1], wc, x0, reverse=True, nb=nb)
    yl_f, _ = _s5_scan(u_lat, wb[0], lam[0], wc, x_f, reverse=False, nb=nb)
    yl_b, _ = _s5_scan(u_lat, wb[1], lam[1], wc, x_b, reverse=True, nb=nb)
    y_lat = _s5_out(yl_f, yl_b, u_lat, d_skip, w_glu, tm=512)
    y_ctx = _s5_out(yc_f, yc_b, u_ctx, d_skip, w_glu, tm=512)
    return y_lat, y_ctx


_NT_DIMS = (((1,), (1,)), ((), ()))
_TN_DIMS = (((0,), (0,)), ((), ()))


def _tri_mask(n, reverse):
    r = lax.broadcasted_iota(jnp.int32, (n, n), 0)
    c = lax.broadcasted_iota(jnp.int32, (n, n), 1)
    return (c >= r) if reverse else (c <= r)


def _cumsum_scan(g, reverse):
    n = g.shape[0]
    rows = lax.broadcasted_iota(jnp.int32, (n, 1), 0)
    d = 1
    while d < n:
        if reverse:
            shifted = jnp.where(rows < n - d, pltpu.roll(g, n - d, axis=0), 0.0)
        else:
            shifted = jnp.where(rows >= d, pltpu.roll(g, d, axis=0), 0.0)
        g = g + shifted
        d *= 2
    return g


def _cumsum_f32(mask_bf16, g):
    g1 = g.astype(BF16)
    r1 = g - g1.astype(F32)
    g2 = r1.astype(BF16)
    g3 = (r1 - g2.astype(F32)).astype(BF16)
    dot = lambda t: jnp.dot(mask_bf16, t, preferred_element_type=F32)
    return dot(g1) + dot(g2) + dot(g3)


def _gla_scan_kernel(q_ref, k_ref, v_ref, glr_ref, wg_ref, bg_ref, s0_ref, o_ref, s_ref, *, reverse):
    @pl.when(pl.program_id(1) == 0)
    def _():
        s_ref[...] = s0_ref[...]

    nb, ch = q_ref.shape[0], q_ref.shape[1]
    mask = _tri_mask(ch, reverse)
    end = 0 if reverse else ch - 1
    glr = glr_ref[...].reshape(nb * ch, LANES).astype(BF16)
    pre_all = jnp.dot(glr, wg_ref[...], preferred_element_type=F32) + bg_ref[...]
    pairs = [(r, h) for r in range(nb) for h in range(GLA_HEADS)]
    sk = lambda h: slice(h * GLA_DK, (h + 1) * GLA_DK)
    sv = lambda h: slice(h * GLA_DV, (h + 1) * GLA_DV)
    q_dec, k_dec, k_end, decay_end = [], [], [], []
    for r in range(nb):
        g = jax.nn.log_sigmoid(pre_all[r * ch:(r + 1) * ch]) / GLA_GATE_NORM
        b = _cumsum_scan(g, reverse)
        b_end = b[end:end + 1, :]
        k = k_ref[r].astype(F32)
        q_dec.append((q_ref[r].astype(F32) * GLA_DK ** -0.5 * jnp.exp(b)).astype(BF16))
        k_dec.append((k * jnp.exp(-b)).astype(BF16))
        k_end.append((k * jnp.exp(b_end - b)).astype(BF16))
        decay_end.append(jnp.exp(b_end))
    scores = [lax.dot_general(q_dec[r][:, sk(h)], k_dec[r][:, sk(h)], _NT_DIMS, preferred_element_type=F32)
              for r, h in pairs]
    scores = [jnp.where(mask, s, 0.0).astype(BF16) for s in scores]
    states = [s_ref[r, h] for r, h in pairs]
    inter = [lax.dot_general(q_dec[r][:, sk(h)], st.astype(BF16), _NT_DIMS, preferred_element_type=F32)
             for (r, h), st in zip(pairs, states)]
    intra = [jnp.dot(s, v_ref[r, :, sv(h)], preferred_element_type=F32) for (r, h), s in zip(pairs, scores)]
    upd = [lax.dot_general(v_ref[r, :, sv(h)], k_end[r][:, sk(h)], _TN_DIMS, preferred_element_type=F32)
           for r, h in pairs]
    for i, (r, h) in enumerate(pairs):
        o_ref[r, :, sv(h)] = (intra[i] + inter[i]).astype(o_ref.dtype)
        s_ref[r, h] = decay_end[r][:, sk(h)] * states[i] + upd[i]


def _gla_scan(z, glr, wg, bg, s0, *, reverse, ch):
    bsz, length, _ = z.shape
    nc = length // ch
    nb = GLA_ROWS if bsz % GLA_ROWS == 0 else 1
    cix = (lambda i: nc - 1 - i) if reverse else (lambda i: i)
    st_spec = pl.BlockSpec((nb, GLA_HEADS, GLA_DV, GLA_DK), lambda b, i: (b, 0, 0, 0))
    shp = jax.ShapeDtypeStruct
    return pl.pallas_call(
        functools.partial(_gla_scan_kernel, reverse=reverse),
        grid=(bsz // nb, nc),
        in_specs=[pl.BlockSpec((nb, ch, GLA_QK), lambda b, i: (b, cix(i), 0)),
                  pl.BlockSpec((nb, ch, GLA_QK), lambda b, i: (b, cix(i), 1)),
                  pl.BlockSpec((nb, ch, GLA_VD), lambda b, i: (b, cix(i), 1)),
                  pl.BlockSpec((nb, ch, LANES), lambda b, i: (b, cix(i), 0)),
                  pl.BlockSpec((LANES, GLA_QK), lambda b, i: (0, 0)),
                  pl.BlockSpec((1, GLA_QK), lambda b, i: (0, 0)),
                  st_spec],
        out_specs=[pl.BlockSpec((nb, ch, GLA_VD), lambda b, i: (b, cix(i), 0)), st_spec],
        out_shape=[shp((bsz, length, GLA_VD), BF16), shp(s0.shape, F32)],
        compiler_params=_params("parallel", "arbitrary"),
        name="gla_scan",
    )(z, z, z, glr, wg, bg, s0)


def _gla_out_kernel(of_ref, ob_ref, r_ref, ng_ref, o_ref):
    for h in range(GLA_HEADS):
        sv = slice(h * GLA_DV, (h + 1) * GLA_DV)
        o = of_ref[0, :, sv].astype(F32) + ob_ref[0, :, sv].astype(F32)
        o = _rms(o, ng_ref[...])
        o_ref[0, :, sv] = (o * _silu(r_ref[0, :, sv].astype(F32))).astype(o_ref.dtype)


def _gla_out(o_f, o_b, z, norm_g, *, tm):
    bsz, length, _ = o_f.shape
    seq = pl.BlockSpec((1, tm, GLA_VD), lambda b, i: (b, i, 0))
    return pl.pallas_call(
        _gla_out_kernel,
        grid=(bsz, length // tm),
        in_specs=[seq, seq, pl.BlockSpec((1, tm, D_MODEL), lambda b, i: (b, i, 2)),
                  pl.BlockSpec((1, GLA_DV), lambda b, i: (0, 0))],
        out_specs=seq,
        out_shape=jax.ShapeDtypeStruct((bsz, length, GLA_VD), BF16),
        compiler_params=_params("parallel", "parallel"),
        name="gla_out",
    )(o_f, o_b, z, norm_g.astype(F32).reshape(1, GLA_DV))


def _gla_weights(w_in, w_gate, b_gate):
    n_main = 2 * GLA_QK + GLA_VD + D_MODEL
    w_main = w_in[:, :n_main].astype(BF16)
    w_glr = jnp.pad(w_in[:, n_main:], ((0, 0), (0, LANES - 2 * GLA_RANK))).astype(BF16)
    wg = jnp.zeros((2, LANES, GLA_QK), F32)
    wg = wg.at[0, :GLA_RANK].set(w_gate[0].astype(F32)).at[1, GLA_RANK:2 * GLA_RANK].set(w_gate[1].astype(F32))
    return w_main, w_glr, wg.astype(BF16), b_gate.astype(F32).reshape(2, 1, GLA_QK)


def _gla_mixer(z_lat, glr_lat, z_ctx, glr_ctx, wg, bg, norm_g):
    bsz = z_lat.shape[0]
    s0 = jnp.zeros((bsz, GLA_HEADS, GLA_DV, GLA_DK), F32)

    def bidirectional(z, glr, s_f, s_b, tm):
        o_f, s_f = _gla_scan(z, glr, wg[0], bg[0], s_f, reverse=False, ch=GLA_CHUNK)
        o_b, s_b = _gla_scan(z, glr, wg[1], bg[1], s_b, reverse=True, ch=GLA_CHUNK)
        return _gla_out(o_f, o_b, z, norm_g, tm=tm), s_f, s_b

    t_ctx, s_f, s_b = bidirectional(z_ctx, glr_ctx, s0, s0, z_ctx.shape[1])
    t_lat, _, _ = bidirectional(z_lat, glr_lat, s_f, s_b, 512)
    return t_lat, t_ctx


def _dot3(a, b):
    a1 = a.astype(BF16)
    a2 = (a - a1.astype(F32)).astype(BF16)
    b1 = b.astype(BF16)
    b2 = (b - b1.astype(F32)).astype(BF16)
    dot = lambda s, t: jnp.dot(s, t, preferred_element_type=F32)
    return dot(a1, b1) + dot(a1, b2) + dot(a2, b1)


def _hy_filter_kernel(bands_ref, w10_ref, w1c_ref, w1s_ref, b1_ref, w2_ref, b2_ref, w3_ref, fr_ref, dl_ref,
                      ha_ref, hb_ref, *, length):
    tm = ha_ref.shape[0]
    n = (pl.program_id(0) * tm + lax.broadcasted_iota(jnp.int32, (tm, 1), 0)).astype(F32)

    def mlp(pos):
        t = pos * (1.0 / length)
        ang = (2.0 * math.pi * t) * bands_ref[...]
        pre = (t * w10_ref[...] + _dot3(jnp.cos(ang), w1c_ref[...]) - _dot3(jnp.sin(ang), w1s_ref[...])
               + b1_ref[...])
        z = jnp.sin(fr_ref[...] * pre)
        z = jnp.sin(fr_ref[...] * (_dot3(z, w2_ref[...]) + b2_ref[...]))
        return z, jnp.exp(-t * dl_ref[...])

    z, dec = mlp(n)
    h_f = _dot3(z, w3_ref[:, 0:D_MODEL]) * dec
    h_b0 = _dot3(z, w3_ref[:, D_MODEL:2 * D_MODEL]) * dec
    ha_ref[...] = (h_f + jnp.where(n == 0.0, h_b0, 0.0)).astype(ha_ref.dtype)
    zb, decb = mlp(length - n)
    h_b = _dot3(zb, w3_ref[:, D_MODEL:2 * D_MODEL]) * decb
    hb_ref[...] = jnp.where(n > 0.0, h_b, 0.0).astype(hb_ref.dtype)


def _hy_filter(length, w1, b1, w2, b2, w3, freq):
    f32 = lambda t: t.astype(F32)
    nb = HY_BANDS
    bands = jnp.pad(jnp.linspace(1e-4, nb - 1, nb, dtype=F32), (0, LANES - nb)).reshape(1, LANES)
    w1 = f32(w1)
    w1c = jnp.pad(w1[1:1 + nb], ((0, LANES - nb), (0, 0)))
    w1s = jnp.pad(w1[1 + nb:1 + 2 * nb], ((0, LANES - nb), (0, 0)))
    log_target = math.log(HY_DECAY_TARGET)
    deltas = jnp.abs(jnp.linspace(log_target / HY_SLOW_DECAY, log_target / HY_FAST_DECAY, D_MODEL, dtype=F32))
    tm = min(length, 512)
    args = [bands, w1[0:1], w1c, w1s, f32(b1).reshape(1, -1), f32(w2), f32(b2).reshape(1, -1), f32(w3),
            f32(freq).reshape(1, -1), deltas.reshape(1, D_MODEL)]
    out_spec = pl.BlockSpec((tm, D_MODEL), lambda i: (i, 0))
    return pl.pallas_call(
        functools.partial(_hy_filter_kernel, length=length),
        grid=(length // tm,),
        in_specs=[pl.BlockSpec(a.shape, lambda i: (0, 0)) for a in args],
        out_specs=[out_spec, out_spec],
        out_shape=[jax.ShapeDtypeStruct((length, D_MODEL), BF16)] * 2,
        compiler_params=_params("parallel"),
        name="hy_filter",
    )(*args)


def _conv3(x_ref, xp_ref, xn_ref, cw_ref, cb_ref, r, nr):
    tm = x_ref.shape[1]
    x = x_ref[0].astype(F32)
    prev_row = jnp.where(r > 0, xp_ref[0][HALO - 1:HALO, :].astype(F32), 0.0)
    next_row = jnp.where(r < nr - 1, xn_ref[0][0:1, :].astype(F32), 0.0)
    rows = lax.broadcasted_iota(jnp.int32, (tm, 1), 0)
    x_prev = jnp.where(rows == 0, prev_row, pltpu.roll(x, 1, axis=0))
    x_next = jnp.where(rows == tm - 1, next_row, pltpu.roll(x, tm - 1, axis=0))
    return cw_ref[0:1, :] * x_prev + cw_ref[1:2, :] * x + cw_ref[2:3, :] * x_next + cb_ref[...]


def _conv3_specs(tm, length, ct, lane_block):
    hb, nhb = tm // HALO, length // HALO
    return [pl.BlockSpec((1, tm, ct), lambda b, r, j: (b, r, lane_block(j))),
            pl.BlockSpec((1, HALO, ct), lambda b, r, j: (b, jnp.maximum(r * hb - 1, 0), lane_block(j))),
            pl.BlockSpec((1, HALO, ct), lambda b, r, j: (b, jnp.minimum((r + 1) * hb, nhb - 1), lane_block(j))),
            pl.BlockSpec((3, ct), lambda b, r, j: (0, lane_block(j))),
            pl.BlockSpec((1, ct), lambda b, r, j: (0, lane_block(j)))]


def _hy_conv_kernel(*refs):
    r, nr = pl.program_id(1), pl.num_programs(1)
    u = [_conv3(*refs[5 * a:5 * a + 5], r, nr) for a in range(3)]
    x0_ref, p_ref = refs[15:]
    x0_ref[0] = u[0].astype(x0_ref.dtype)
    p_ref[0] = (u[2] * u[1]).astype(p_ref.dtype)


def _hy_conv(z, conv_w, conv_b, *, tm):
    bsz, length, _ = z.shape
    nj = D_MODEL // HY_CT
    cw, cb = conv_w.astype(F32), conv_b.astype(F32).reshape(1, -1)
    in_specs, args = [], []
    for a in range(3):
        in_specs += _conv3_specs(tm, length, HY_CT, lambda j, a=a: a * nj + j)
        args += [z, z, z, cw, cb]
    out_spec = pl.BlockSpec((1, tm, HY_CT), lambda b, r, j: (b, r, j))
    return pl.pallas_call(
        _hy_conv_kernel,
        grid=(bsz, length // tm, nj),
        in_specs=in_specs,
        out_specs=[out_spec, out_spec],
        out_shape=[jax.ShapeDtypeStruct((bsz, length, D_MODEL), BF16)] * 2,
        compiler_params=_params("parallel", "parallel", "parallel"),
        name="hy_conv",
    )(*args)


def _dft_cs(rows, cols, modulus):
    m = (rows.astype(jnp.int32)[:, None] * cols.astype(jnp.int32)[None, :]) % modulus
    theta = m.astype(F32) * (2.0 * math.pi / modulus)
    return jnp.cos(theta), jnp.sin(theta)


def _cplx_rows(c, s):
    return jnp.concatenate([jnp.concatenate([c, s], axis=1), jnp.concatenate([-s, c], axis=1)], axis=0)


def _hy_s1_kernel(*refs, real):
    if real:
        zr_ref, f_ref, o_ref = refs
        z = zr_ref[0]
    else:
        zr_ref, zi_ref, f_ref, o_ref = refs
        z = jnp.concatenate([zr_ref[0], zi_ref[0]], axis=0)
    res = jnp.dot(f_ref[...], z, preferred_element_type=F32)
    o_ref[0, 0] = res[:HY_N1].astype(o_ref.dtype)
    o_ref[0, 1] = res[HY_N1:].astype(o_ref.dtype)


def _hy_stage1(p, f, *, real):
    nseq, length, d = p.shape
    n1h = HY_N1 // 2
    cols = (length // n1h) * d
    pv = p.reshape(nseq, n1h, cols)
    npair = nseq if real else nseq // 2
    tc = HY_TC
    in_specs = [pl.BlockSpec((1, n1h, tc), lambda q, j: (q, 0, j))]
    args = [pv]
    if not real:
        in_specs.append(pl.BlockSpec((1, n1h, tc), lambda q, j: (q + npair, 0, j)))
        args.append(pv)
    in_specs.append(pl.BlockSpec(f.shape, lambda q, j: (0, 0)))
    return pl.pallas_call(
        functools.partial(_hy_s1_kernel, real=real),
        grid=(npair, cols // tc),
        in_specs=in_specs,
        out_specs=pl.BlockSpec((1, 2, HY_N1, tc), lambda q, j: (q, 0, 0, j)),
        out_shape=jax.ShapeDtypeStruct((npair, 2, HY_N1, cols), BF16),
        compiler_params=_params("parallel", "parallel"),
        name="hy_stage1",
    )(*args, f)


def _hy_s2f_kernel(a_ref, g_ref, o_ref):
    a = jnp.concatenate([a_ref[0, 0, 0], a_ref[0, 1, 0]], axis=0)
    o_ref[0, 0] = jnp.dot(g_ref[0], a, preferred_element_type=F32)


def _hy_stage2_filter(a, g):
    nseq = a.shape[0]
    av = a.reshape(nseq, 2, HY_N1, HY_N2, D_MODEL)
    return pl.pallas_call(
        _hy_s2f_kernel,
        grid=(HY_N1, nseq),
        in_specs=[pl.BlockSpec((1, 2, 1, HY_N2, D_MODEL), lambda k, q: (q, 0, k, 0, 0)),
                  pl.BlockSpec((1, 2 * HY_N2, 2 * HY_N2), lambda k, q: (k, 0, 0))],
        out_specs=pl.BlockSpec((1, 1, 2 * HY_N2, D_MODEL), lambda k, q: (q, k, 0, 0)),
        out_shape=jax.ShapeDtypeStruct((nseq, HY_N1, 2 * HY_N2, D_MODEL), F32),
        compiler_params=_params("parallel", "parallel"),
        name="hy_stage2_filter",
    )(av, g)


def _cmul(x, h, n):
    xr, xi, hr, hi = x[:n], x[n:], h[:n], h[n:]
    return jnp.concatenate([xr * hr - xi * hi, xr * hi + xi * hr], axis=0)


def _hy_s2_kernel(a_ref, g_ref, gi_ref, h_ref, o_ref):
    sign = (1 - 2 * (pl.program_id(0) % 2)).astype(F32)
    h = h_ref[0, 0] + sign * h_ref[1, 0]
    pairs = range(a_ref.shape[0])
    x = [jnp.dot(g_ref[0], jnp.concatenate([a_ref[q, 0, 0], a_ref[q, 1, 0]], axis=0),
                 preferred_element_type=F32) for q in pairs]
    y = [_cmul(x[q], h, HY_N2).astype(BF16) for q in pairs]
    b = [jnp.dot(gi_ref[0], y[q], preferred_element_type=F32) for q in pairs]
    for q in pairs:
        o_ref[q, 0, 0] = b[q][:HY_N2].astype(o_ref.dtype)
        o_ref[q, 1, 0] = b[q][HY_N2:].astype(o_ref.dtype)


def _hy_stage2(a, g, gi, hspec):
    npair = a.shape[0]
    av = a.reshape(npair, 2, HY_N1, HY_N2, D_MODEL)
    blk = pl.BlockSpec((npair, 2, 1, HY_N2, D_MODEL), lambda k: (0, 0, k, 0, 0))
    mat = pl.BlockSpec((1, 2 * HY_N2, 2 * HY_N2), lambda k: (k, 0, 0))
    out = pl.pallas_call(
        _hy_s2_kernel,
        grid=(HY_N1,),
        in_specs=[blk, mat, mat,
                  pl.BlockSpec((2, 1, 2 * HY_N2, D_MODEL), lambda k: (0, k, 0, 0))],
        out_specs=blk,
        out_shape=jax.ShapeDtypeStruct(av.shape, BF16),
        compiler_params=_params("parallel"),
        name="hy_stage2",
    )(av, g, gi, hspec)
    return out.reshape(a.shape)


def _hy_s1inv_kernel(b_ref, f_ref, x0_ref, p_ref, bias_ref, o_ref):
    bc = jnp.concatenate([b_ref[0, 0], b_ref[0, 1]], axis=0)
    y = jnp.dot(f_ref[...], bc, preferred_element_type=F32)
    n1h = HY_N1 // 2
    for half in range(2):
        conv = y[half * n1h:(half + 1) * n1h]
        u = p_ref[half, 0].astype(F32)
        o_ref[half, 0] = (x0_ref[half, 0].astype(F32) * (conv + u * bias_ref[...])).astype(o_ref.dtype)


def _hy_stage1_inv(b, f, x0, p, bias):
    npair, _, _, cols = b.shape
    bsz, length, d = x0.shape
    n1h = HY_N1 // 2
    tc = HY_TC
    view = lambda t: t.reshape(2, npair, n1h, cols)
    seq = pl.BlockSpec((2, 1, n1h, tc), lambda q, j: (0, q, 0, j))
    out = pl.pallas_call(
        _hy_s1inv_kernel,
        grid=(npair, cols // tc),
        in_specs=[pl.BlockSpec((1, 2, HY_N1, tc), lambda q, j: (q, 0, 0, j)),
                  pl.BlockSpec(f.shape, lambda q, j: (0, 0)),
                  seq, seq,
                  pl.BlockSpec((1, tc), lambda q, j: (0, 0))],
        out_specs=seq,
        out_shape=jax.ShapeDtypeStruct((2, npair, n1h, cols), BF16),
        compiler_params=_params("parallel", "parallel"),
        name="hy_stage1_inv",
    )(b, f, view(x0), view(p), jnp.tile(bias.astype(F32).reshape(1, d), (1, tc // d)))
    return out.reshape(bsz, length, d)


def _hy_ctx_filter_kernel(ha_ref, hb_ref, f_ref, o_ref):
    h2 = jnp.concatenate([ha_ref[...], hb_ref[...]], axis=0)
    o_ref[...] = jnp.dot(f_ref[...], h2, preferred_element_type=F32)


def _hy_ctx_kernel(zr_ref, zi_ref, f_ref, fi_ref, h_ref, x0_ref, p_ref, bias_ref, o_ref):
    length = zr_ref.shape[1]
    z = jnp.concatenate([zr_ref[0], zi_ref[0]], axis=0)
    x = jnp.dot(f_ref[...], z, preferred_element_type=F32)
    y = _cmul(x, h_ref[...], 2 * length).astype(BF16)
    out = jnp.dot(fi_ref[...], y, preferred_element_type=F32)
    for half in range(2):
        conv = out[half * length:(half + 1) * length]
        u = p_ref[half, 0].astype(F32)
        o_ref[half, 0] = (x0_ref[half, 0].astype(F32) * (conv + u * bias_ref[...])).astype(o_ref.dtype)


def _hy_ctx_conv(x0, p, ha, hb, bias):
    bsz, length, d = p.shape
    npair = bsz // 2
    n = 2 * length
    k = jnp.arange(n)
    c, s = _dft_cs(k, jnp.arange(length), n)
    f = _cplx_rows(c, s).astype(BF16)
    fi = (_cplx_rows(c.T, -s.T) * (1.0 / n)).astype(BF16)
    cf, sf = _dft_cs(k, k, n)
    ffilt = jnp.concatenate([cf, -sf], axis=0).astype(BF16)
    full = lambda t: pl.BlockSpec(t.shape, lambda *_: (0,) * t.ndim)
    hspec = pl.pallas_call(
        _hy_ctx_filter_kernel,
        out_shape=jax.ShapeDtypeStruct((2 * n, d), F32),
        compiler_params=_params(),
        name="hy_ctx_filter",
    )(ha, hb, ffilt)
    view = lambda t: t.reshape(2, npair, length, d)
    seq = pl.BlockSpec((2, 1, length, d), lambda q: (0, q, 0, 0))
    bias2 = bias.astype(F32).reshape(1, d)
    out = pl.pallas_call(
        _hy_ctx_kernel,
        grid=(npair,),
        in_specs=[pl.BlockSpec((1, length, d), lambda q: (q, 0, 0)),
                  pl.BlockSpec((1, length, d), lambda q: (q + npair, 0, 0)),
                  full(f), full(fi), full(hspec), seq, seq, full(bias2)],
        out_specs=seq,
        out_shape=jax.ShapeDtypeStruct((2, npair, length, d), BF16),
        compiler_params=_params("parallel"),
        name="hy_ctx_conv",
    )(p, p, f, fi, hspec, view(x0), view(p), bias2)
    return out.reshape(bsz, length, d)


def _hy_lat_conv(x0, p, ha, hb, bias):
    bsz, length, d = p.shape
    assert 2 * length == HY_N1 * HY_N2 and d == D_MODEL and bsz % 2 == 0
    n = 2 * length
    k1, n1 = jnp.arange(HY_N1), jnp.arange(HY_N1 // 2)
    c1, s1 = _dft_cs(k1, n1, HY_N1)
    f1 = _cplx_rows(c1, s1).astype(BF16)
    f1_real = jnp.concatenate([c1, -s1], axis=0).astype(BF16)
    f1_inv = (_cplx_rows(c1.T, -s1.T) * (1.0 / n)).astype(BF16)
    k = (k1[:, None] + HY_N1 * jnp.arange(HY_N2)[None, :]).reshape(-1)
    c2, s2 = _dft_cs(k, jnp.arange(HY_N2), n)
    c2, s2 = c2.reshape(HY_N1, HY_N2, HY_N2), s2.reshape(HY_N1, HY_N2, HY_N2)
    g = jax.vmap(_cplx_rows)(c2, s2).astype(BF16)
    gi = jax.vmap(_cplx_rows)(c2.transpose(0, 2, 1), -s2.transpose(0, 2, 1)).astype(BF16)
    hfilt = jnp.stack([ha, hb], axis=0)
    hspec = _hy_stage2_filter(_hy_stage1(hfilt, f1_real, real=True), g)
    a = _hy_stage1(p, f1, real=False)
    b = _hy_stage2(a, g, gi, hspec)
    return _hy_stage1_inv(b, f1_inv, x0, p, bias)


def _hy_mixer(z_lat, z_ctx, conv_w, conv_b, f_w1, f_b1, f_w2, f_b2, f_w3, f_freq, f_bias):
    def run(z, tm, conv_fn):
        length = z.shape[1]
        x0, p = _hy_conv(z, conv_w, conv_b, tm=tm)
        ha, hb = _hy_filter(length, f_w1, f_b1, f_w2, f_b2, f_w3, f_freq)
        return conv_fn(x0, p, ha, hb, f_bias)

    return run(z_lat, 1024, _hy_lat_conv), run(z_ctx, z_ctx.shape[1], _hy_ctx_conv)


def _ml_qkv_kernel(xm_ref, xp_ref, xn_ref, cw_ref, cb_ref, wq_ref, wk_ref, wv_ref, wg_ref, bg_ref,
                   q_ref, k_ref, v_ref, xc_ref, g_ref):
    j = pl.program_id(2)
    xm = xm_ref[0]
    conv = _conv3(xm_ref, xp_ref, xn_ref, cw_ref, cb_ref, pl.program_id(1), pl.num_programs(1))
    xc = _silu(conv).astype(BF16)
    subs = [slice(s * ML_BD, (s + 1) * ML_BD) for s in range(ML_CT // ML_BD)]
    bd = lambda x, w_ref: jnp.concatenate(
        [jnp.dot(x[:, sl], w_ref[s], preferred_element_type=F32) for s, sl in enumerate(subs)], axis=1)
    q, k, v = bd(xc, wq_ref), bd(xc, wk_ref), bd(xm, wv_ref)
    q, k, v = q.astype(BF16), k.astype(BF16), v.astype(BF16)
    q_ref[0], k_ref[0], v_ref[0], xc_ref[0] = q, k, v, xc
    contrib = (jnp.dot(q, wg_ref[0], preferred_element_type=F32)
               + jnp.dot(k, wg_ref[1], preferred_element_type=F32)
               + jnp.dot(v, wg_ref[2], preferred_element_type=F32))

    @pl.when(j == 0)
    def _():
        g_ref[0] = contrib

    @pl.when(j > 0)
    def _():
        g_ref[0] += contrib

    @pl.when(j == pl.num_programs(2) - 1)
    def _():
        g = g_ref[0] + bg_ref[...]
        lane = lax.broadcasted_iota(jnp.int32, g.shape, 1)
        g_ref[0] = jnp.where(lane % (2 * ML_HEADS) >= ML_HEADS, jax.nn.log_sigmoid(g), g)


def _ml_qkv(z, conv_w, conv_b, wq, wk, wv, wg, bg, *, tm):
    bsz, length, _ = z.shape
    nj = ML_INNER // ML_CT
    seq = lambda: pl.BlockSpec((1, tm, ML_CT), lambda b, r, j: (b, r, j))
    shp = jax.ShapeDtypeStruct
    return pl.pallas_call(
        _ml_qkv_kernel,
        grid=(bsz, length // tm, nj),
        in_specs=_conv3_specs(tm, length, ML_CT, lambda j: j) + [
                  pl.BlockSpec((ML_CT // ML_BD, ML_BD, ML_BD), lambda b, r, j: (j, 0, 0)),
                  pl.BlockSpec((ML_CT // ML_BD, ML_BD, ML_BD), lambda b, r, j: (j, 0, 0)),
                  pl.BlockSpec((ML_CT // ML_BD, ML_BD, ML_BD), lambda b, r, j: (j, 0, 0)),
                  pl.BlockSpec((3, ML_CT, LANES), lambda b, r, j: (0, j, 0)),
                  pl.BlockSpec((1, LANES), lambda b, r, j: (0, 0))],
        out_specs=[seq(), seq(), seq(), seq(),
                   pl.BlockSpec((1, tm, LANES), lambda b, r, j: (b, r, 0))],
        out_shape=[shp((bsz, length, ML_INNER), BF16)] * 4 + [shp((bsz, length, LANES), F32)],
        compiler_params=_params("parallel", "parallel", "arbitrary"),
        name="ml_qkv",
    )(z, z, z, conv_w, conv_b, wq, wk, wv, wg, bg)


def _ml_scan_kernel(q_ref, k_ref, v_ref, g_ref, c0_ref, n0_ref, m0_ref, h_ref, c_ref, n_ref, m_ref,
                    *, reverse, z):
    @pl.when(pl.program_id(1) == 0)
    def _():
        c_ref[...] = c0_ref[...]
        n_ref[...] = n0_ref[...]
        m_ref[...] = m0_ref[...]

    ch = q_ref.shape[1]
    mask = _tri_mask(ch, reverse)
    gates = g_ref[0]
    bcum = _cumsum_f32(mask.astype(BF16), gates)
    gates_t, bcum_t = gates.T, bcum.T
    end = 0 if reverse else ch - 1
    for h in range(ML_HEADS):
        li, lf = z * 2 * ML_HEADS + h, z * 2 * ML_HEADS + ML_HEADS + h
        sl = slice(h * ML_DH, (h + 1) * ML_DH)
        q, v = q_ref[0, :, sl], v_ref[0, :, sl]
        k = k_ref[0, :, sl].astype(F32) * ML_DH ** -0.5
        b_col, b_row = bcum[:, lf:lf + 1], bcum_t[lf:lf + 1, :]
        i_col, i_row = gates[:, li:li + 1], gates_t[li:li + 1, :]
        b_end = b_col[end:end + 1, :]
        m_prev = m_ref[0, h][0:1, 0:1]
        d_intra = jnp.where(mask, b_col - b_row + i_row, -jnp.inf)
        d_inter = b_col + m_prev
        m_tok = jnp.maximum(jnp.max(d_intra, axis=1, keepdims=True), d_inter)
        w_inter = jnp.exp(d_inter - m_tok)
        s = lax.dot_general(q, k.astype(BF16), _NT_DIMS, preferred_element_type=F32) * jnp.exp(d_intra - m_tok)
        c_mat = c_ref[0, h]
        n_vec = n_ref[0, h]
        num = (jnp.dot(s.astype(BF16), v, preferred_element_type=F32)
               + w_inter * jnp.dot(q, c_mat.astype(BF16), preferred_element_type=F32))
        den = (jnp.sum(s, axis=1, keepdims=True)
               + w_inter * jnp.sum(q.astype(F32) * n_vec, axis=1, keepdims=True))
        h_ref[0, :, sl] = (num / jnp.maximum(jnp.abs(den), jnp.exp(-m_tok))).astype(h_ref.dtype)
        d_state = b_end - b_col + i_col
        m_new = jnp.maximum(b_end + m_prev, jnp.max(d_state, axis=0, keepdims=True))
        w_prev = jnp.exp(b_end + m_prev - m_new)
        kw = k * jnp.exp(d_state - m_new)
        c_ref[0, h] = w_prev * c_mat + lax.dot_general(kw.astype(BF16), v, _TN_DIMS, preferred_element_type=F32)
        n_ref[0, h] = w_prev * n_vec + jnp.sum(kw, axis=0, keepdims=True)
        m_ref[0, h] = jnp.broadcast_to(m_new, m_ref.shape[2:])


def _ml_scan(q, k, v, gates, state, *, reverse, z, ch):
    bsz, length, _ = q.shape
    nc = length // ch
    cix = (lambda i: nc - 1 - i) if reverse else (lambda i: i)
    seq = lambda w: pl.BlockSpec((1, ch, w), lambda b, i: (b, cix(i), 0))
    st_specs = [pl.BlockSpec((1,) + s.shape[1:], lambda b, i: (b, 0, 0, 0)) for s in state]
    shp = jax.ShapeDtypeStruct
    out = pl.pallas_call(
        functools.partial(_ml_scan_kernel, reverse=reverse, z=z),
        grid=(bsz, nc),
        in_specs=[seq(ML_INNER), seq(ML_INNER), seq(ML_INNER), seq(LANES)] + st_specs,
        out_specs=[seq(ML_INNER)] + st_specs,
        out_shape=[shp((bsz, length, ML_INNER), BF16)] + [shp(s.shape, F32) for s in state],
        compiler_params=_params("parallel", "arbitrary"),
        name="ml_scan",
    )(q, k, v, gates, *state)
    return out[0], tuple(out[1:])


def _ml_out_kernel(hf_ref, hb_ref, xc_ref, z_ref, ng_ref, sk_ref, o_ref):
    for h in range(ML_HEADS):
        sl = slice(h * ML_DH, (h + 1) * ML_DH)
        o = hf_ref[0, :, sl].astype(F32) + hb_ref[0, :, sl].astype(F32)
        o = o - jnp.mean(o, axis=-1, keepdims=True)
        o = o * lax.rsqrt(jnp.mean(o * o, axis=-1, keepdims=True) + EPS) * ng_ref[:, sl]
        t = (o + sk_ref[:, sl] * xc_ref[0, :, sl].astype(F32)) * _silu(z_ref[0, :, sl].astype(F32))
        o_ref[0, :, sl] = t.astype(o_ref.dtype)


def _ml_out(h_f, h_b, xc, z, norm_g, skip, *, tm):
    bsz, length, _ = h_f.shape
    seq = pl.BlockSpec((1, tm, ML_INNER), lambda b, i: (b, i, 0))
    vec = pl.BlockSpec((1, ML_INNER), lambda b, i: (0, 0))
    return pl.pallas_call(
        _ml_out_kernel,
        grid=(bsz, length // tm),
        in_specs=[seq, seq, seq, pl.BlockSpec((1, tm, ML_INNER), lambda b, i: (b, i, 1)), vec, vec],
        out_specs=seq,
        out_shape=jax.ShapeDtypeStruct((bsz, length, ML_INNER), BF16),
        compiler_params=_params("parallel", "parallel"),
        name="ml_out",
    )(h_f, h_b, xc, z, norm_g.astype(F32).reshape(1, ML_INNER), skip.astype(F32).reshape(1, ML_INNER))


def _ml_weights(w_q, w_k, w_v, w_gates, b_gates):
    nb = ML_BD // ML_BLOCK
    eye = jnp.eye(nb, dtype=F32)

    def dense(w):
        w = w.astype(F32).reshape(ML_INNER // ML_BD, nb, ML_BLOCK, ML_BLOCK)
        return jnp.einsum('tncd,nm->tncmd', w, eye).reshape(ML_INNER // ML_BD, ML_BD, ML_BD).astype(BF16)

    ng = 2 * 2 * ML_HEADS
    wg = w_gates.astype(F32).reshape(2, 3, ML_INNER, 2 * ML_HEADS).transpose(1, 2, 0, 3).reshape(3, ML_INNER, ng)
    wg = jnp.pad(wg, ((0, 0), (0, 0), (0, LANES - ng))).astype(BF16)
    bg = jnp.pad(b_gates.astype(F32).reshape(1, ng), ((0, 0), (0, LANES - ng)))
    return dense(w_q), dense(w_k), dense(w_v), wg, bg


def _ml_mixer(z_lat, z_ctx, conv_w, conv_b, w_q, w_k, w_v, w_gates, b_gates, norm_g, skip):
    wq, wk, wv, wg, bg = _ml_weights(w_q, w_k, w_v, w_gates, b_gates)
    cw, cb = conv_w.astype(F32), conv_b.astype(F32).reshape(1, ML_INNER)
    bsz = z_lat.shape[0]
    state = (jnp.zeros((bsz, ML_HEADS, ML_DH, ML_DH), F32),
             jnp.zeros((bsz, ML_HEADS, 1, ML_DH), F32),
             jnp.full((bsz, ML_HEADS, 8, LANES), -jnp.inf, F32))

    def bidirectional(z, st_f, st_b, tm):
        q, k, v, xc, gates = _ml_qkv(z, cw, cb, wq, wk, wv, wg, bg, tm=min(2 * tm, z.shape[1]))
        h_f, st_f = _ml_scan(q, k, v, gates, st_f, reverse=False, z=0, ch=ML_CHUNK)
        h_b, st_b = _ml_scan(q, k, v, gates, st_b, reverse=True, z=1, ch=ML_CHUNK)
        return _ml_out(h_f, h_b, xc, z, norm_g, skip, tm=tm), st_f, st_b

    t_ctx, st_f, st_b = bidirectional(z_ctx, state, state, z_ctx.shape[1])
    t_lat, _, _ = bidirectional(z_lat, st_f, st_b, 512)
    return t_lat, t_ctx


def kernel(x, c, ctx, c_ctx, mod_w, mod_b, norm_g, ffn_w_in, ffn_w_out, gla_w_in, gla_w_gate, gla_b_gate, gla_norm_g, gla_w_out, hy_w_in, hy_conv_w, hy_conv_b, hy_f_w1, hy_f_b1, hy_f_w2, hy_f_b2, hy_f_w3, hy_f_freq, hy_f_bias, hy_w_out, ml_w_in, ml_conv_w, ml_conv_b, ml_w_q, ml_w_k, ml_w_v, ml_w_gates, ml_b_gates, ml_norm_g, ml_skip, ml_w_out, s5_lam_re, s5_lam_im, s5_log_dt, s5_b_re, s5_b_im, s5_c_re, s5_c_im, s5_d, s5_w_glu):
    bsz, seq, d = x.shape
    ctx_len = ctx.shape[1]
    depth = mod_w.shape[0]
    mods = _modulation(c, c_ctx, mod_w, mod_b)
    lat = x.astype(F32)
    cx = ctx.astype(F32)
    tm_lat, tm_ctx = 512, ctx_len
    lat_col_major = False
    for i in range(depth):
        kind, j = i % N_MIXERS, i // N_MIXERS
        last = i == depth - 1
        if (kind >= 2) != lat_col_major:
            lat = _from_col_major(lat) if lat_col_major else _to_col_major(lat)
            lat_col_major = not lat_col_major
        mod_l = mods[i, :bsz]
        mod_c = mods[i, bsz:bsz + 1]
        win, wout = _ffn_weights(ffn_w_in[i], ffn_w_out[i])
        wmix = None
        tmajor = False
        if kind == 3:
            tmajor = True
            u_lat = _norm_mod(lat, mod_l, norm_g[i, 0], tm=tm_lat, tmajor=True)
            u_ctx = _norm_mod(cx, mod_c, norm_g[i, 0], tm=tm_ctx, tmajor=True)
            y_lat, y_ctx = _s5_mixer(u_lat.reshape(1, seq * bsz, d), u_ctx.reshape(1, ctx_len * bsz, d), bsz,
                                     s5_lam_re[j], s5_lam_im[j], s5_log_dt[j], s5_b_re[j], s5_b_im[j],
                                     s5_c_re[j], s5_c_im[j], s5_d[j], s5_w_glu[j])
            y_lat = y_lat.reshape(1, seq, bsz * d)
            y_ctx = y_ctx.reshape(1, ctx_len, bsz * d)
        elif kind == 0:
            w_main, w_glr, wg, bg = _gla_weights(gla_w_in[j], gla_w_gate[j], gla_b_gate[j])
            z_lat, glr_lat = _norm_proj(lat, mod_l, norm_g[i, 0], [w_main, w_glr], [BF16, F32], tm=tm_lat)
            z_ctx, glr_ctx = _norm_proj(cx, mod_c, norm_g[i, 0], [w_main, w_glr], [BF16, F32], tm=tm_ctx)
            y_lat, y_ctx = _gla_mixer(z_lat, glr_lat, z_ctx, glr_ctx, wg, bg, gla_norm_g[j])
            wmix = gla_w_out[j].astype(BF16)
        elif kind == 2:
            w_in = ml_w_in[j].astype(BF16)
            (z_lat,) = _norm_proj(lat, mod_l, norm_g[i, 0], [w_in], [BF16], tm=tm_lat)
            (z_ctx,) = _norm_proj(cx, mod_c, norm_g[i, 0], [w_in], [BF16], tm=tm_ctx)
            y_lat, y_ctx = _ml_mixer(z_lat, z_ctx, ml_conv_w[j], ml_conv_b[j], ml_w_q[j], ml_w_k[j], ml_w_v[j],
                                     ml_w_gates[j], ml_b_gates[j], ml_norm_g[j], ml_skip[j])
            wmix = ml_w_out[j].astype(BF16)
        else:
            w_in = hy_w_in[j].astype(BF16)
            (z_lat,) = _norm_proj(lat, mod_l, norm_g[i, 0], [w_in], [BF16], tm=tm_lat)
            (z_ctx,) = _norm_proj(cx, mod_c, norm_g[i, 0], [w_in], [BF16], tm=tm_ctx)
            y_lat, y_ctx = _hy_mixer(z_lat, z_ctx, hy_conv_w[j], hy_conv_b[j], hy_f_w1[j], hy_f_b1[j],
                                     hy_f_w2[j], hy_f_b2[j], hy_f_w3[j], hy_f_freq[j], hy_f_bias[j])
            wmix = hy_w_out[j].astype(BF16)
        lat = _post_ffn(lat, y_lat, mod_l, norm_g[i], wmix, win, wout, tm=tm_lat, tmajor=tmajor)
        if not last:
            cx = _post_ffn(cx, y_ctx, mod_c, norm_g[i], wmix, win, wout, tm=tm_ctx, tmajor=tmajor)
    return _from_col_major(lat) if lat_col_major else lat
```

```python
import functools
import math

import jax
import jax.numpy as jnp
from jax import lax
from jax.experimental import pallas as pl
from jax.experimental.pallas import tpu as pltpu

F32 = jnp.float32
BF16 = jnp.bfloat16

D_MODEL = 1024
GRID_W = 64
N_MOD = 6
EPS = 1e-6
FFN_HIDDEN = 2816
FFN_CHUNK = 256
N_FFN_CHUNKS = FFN_HIDDEN // FFN_CHUNK
MOD_ROWS = 16
VMEM_LIMIT = 56 * 1024 * 1024
LANES = 128
N_MIXERS = 4
PROJ_TN = 512

GLA_HEADS = 4
GLA_QK = D_MODEL // 2
GLA_VD = D_MODEL
GLA_DK = GLA_QK // GLA_HEADS
GLA_DV = GLA_VD // GLA_HEADS
GLA_RANK = 16
GLA_GATE_NORM = 16.0
GLA_CHUNK = 64
GLA_ROWS = 8

HALO = 16
HY_BANDS = 16
HY_DECAY_TARGET = 1e-2
HY_FAST_DECAY = 0.3
HY_SLOW_DECAY = 1.5
HY_CT = 256
HY_N1, HY_N2 = 64, 128
HY_TC = 8192

ML_HEADS = 4
ML_INNER = 2 * D_MODEL
ML_DH = ML_INNER // ML_HEADS
ML_BLOCK = 4
ML_CHUNK = 256
ML_CT = 512
ML_BD = 256

S5_GROUP = 16
S5_GROUPS = D_MODEL // S5_GROUP
S5_STATE = 64
S5_TILE_G = LANES // S5_GROUP
S5_NT = S5_GROUPS // S5_TILE_G
S5_SL = S5_TILE_G * S5_STATE
S5_TB = 64
S5_KT = 4


def _rms(x, g):
    return x * lax.rsqrt(jnp.mean(x * x, axis=-1, keepdims=True) + EPS) * g


def _silu(x):
    return x * jax.nn.sigmoid(x)


def _params(*sem):
    return pltpu.CompilerParams(dimension_semantics=sem, vmem_limit_bytes=VMEM_LIMIT)


def _mod_kernel(cc_ref, w_ref, b_ref, o_ref):
    a = _silu(cc_ref[...])
    o_ref[0] = jnp.dot(a.astype(BF16), w_ref[0].astype(BF16), preferred_element_type=F32) + b_ref[0]


def _modulation(c, c_ctx, mod_w, mod_b):
    depth, d, n = mod_w.shape
    bsz = c.shape[0]
    cc = jnp.concatenate([c.astype(F32), c_ctx.astype(F32)[None],
                          jnp.zeros((MOD_ROWS - bsz - 1, d), F32)], axis=0)
    tn = 1536
    out = pl.pallas_call(
        _mod_kernel,
        grid=(depth, n // tn),
        in_specs=[pl.BlockSpec((MOD_ROWS, d), lambda i, j: (0, 0)),
                  pl.BlockSpec((1, d, tn), lambda i, j: (i, 0, j)),
                  pl.BlockSpec((1, 1, tn), lambda i, j: (i, 0, j))],
        out_specs=pl.BlockSpec((1, MOD_ROWS, tn), lambda i, j: (i, 0, j)),
        out_shape=jax.ShapeDtypeStruct((depth, MOD_ROWS, n), F32),
        compiler_params=_params("parallel", "parallel"),
        name="adaln_mod",
    )(cc, mod_w, mod_b.reshape(depth, 1, n))
    return out.reshape(depth, MOD_ROWS, N_MOD, d)


def _to_col_major(x):
    bsz, length, d = x.shape
    return x.reshape(bsz, length // GRID_W, GRID_W, d).transpose(0, 2, 1, 3).reshape(bsz, length, d)


def _from_col_major(x):
    bsz, length, d = x.shape
    return x.reshape(bsz, GRID_W, length // GRID_W, d).transpose(0, 2, 1, 3).reshape(bsz, length, d)


def _seq_spec(tmajor, tm, width):
    if tmajor:
        return pl.BlockSpec((1, tm, width), lambda b, i: (0, i, b))
    return pl.BlockSpec((1, tm, width), lambda b, i: (b, i, 0))


def _mod_spec(mod):
    if mod.shape[0] > 1:
        return pl.BlockSpec((1, N_MOD, D_MODEL), lambda b, i: (b, 0, 0))
    return pl.BlockSpec((1, N_MOD, D_MODEL), lambda b, i: (0, 0, 0))


def _norm_kernel(x_ref, m_ref, g_ref, o_ref):
    h = _rms(x_ref[0], g_ref[...]) * (1.0 + m_ref[0, 1:2, :]) + m_ref[0, 0:1, :]
    o_ref[0] = h.astype(o_ref.dtype)


def _norm_mod(x, mod, g, *, tm, tmajor=False, out_dtype=F32):
    bsz, length, d = x.shape
    out_shape = (1, length, bsz * d) if tmajor else (bsz, length, d)
    return pl.pallas_call(
        _norm_kernel,
        grid=(bsz, length // tm),
        in_specs=[_seq_spec(False, tm, d), _mod_spec(mod), pl.BlockSpec((1, d), lambda b, i: (0, 0))],
        out_specs=_seq_spec(tmajor, tm, d),
        out_shape=jax.ShapeDtypeStruct(out_shape, out_dtype),
        compiler_params=_params("parallel", "parallel"),
        name="norm_mod",
    )(x, mod, g.reshape(1, d))


def _proj_kernel(*refs, n_w):
    x_ref, m_ref, g_ref = refs[:3]
    w_refs, o_refs = refs[3:3 + n_w], refs[3 + n_w:]
    h = (_rms(x_ref[0], g_ref[...]) * (1.0 + m_ref[0, 1:2, :]) + m_ref[0, 0:1, :]).astype(BF16)
    for w_ref, o_ref in zip(w_refs, o_refs):
        n = w_ref.shape[1]
        tn = min(n, PROJ_TN)
        for n0 in range(0, n, tn):
            o_ref[0, :, n0:n0 + tn] = jnp.dot(h, w_ref[:, n0:n0 + tn],
                                              preferred_element_type=F32).astype(o_ref.dtype)


def _norm_proj(x, mod, g, ws, dtypes, *, tm):
    bsz, length, d = x.shape
    full = lambda shape: pl.BlockSpec(shape, lambda b, i: (0,) * len(shape))
    return pl.pallas_call(
        functools.partial(_proj_kernel, n_w=len(ws)),
        grid=(bsz, length // tm),
        in_specs=[_seq_spec(False, tm, d), _mod_spec(mod), full((1, d))] + [full(w.shape) for w in ws],
        out_specs=[_seq_spec(False, tm, w.shape[1]) for w in ws],
        out_shape=[jax.ShapeDtypeStruct((bsz, length, w.shape[1]), dt) for w, dt in zip(ws, dtypes)],
        compiler_params=_params("parallel", "parallel"),
        name="norm_proj",
    )(x, mod, g.reshape(1, d), *ws)


def _ffn_kernel(*refs, has_wmix):
    if has_wmix:
        x_ref, y_ref, m_ref, g_ref, wmix_ref, win_ref, wout_ref, o_ref = refs
        y = jnp.dot(y_ref[0], wmix_ref[...], preferred_element_type=F32)
    else:
        x_ref, y_ref, m_ref, g_ref, win_ref, wout_ref, o_ref = refs
        y = y_ref[0].astype(F32)
    lat = x_ref[0] + m_ref[0, 2:3, :] * _rms(y, g_ref[1:2, :])
    h = (_rms(lat, g_ref[2:3, :]) * (1.0 + m_ref[0, 4:5, :]) + m_ref[0, 3:4, :]).astype(BF16)

    def body(c, acc):
        gate = jnp.dot(h, win_ref[0, c], preferred_element_type=F32)
        up = jnp.dot(h, win_ref[1, c], preferred_element_type=F32)
        a = (_silu(gate) * up).astype(BF16)
        return acc + jnp.dot(a, wout_ref[c], preferred_element_type=F32)

    acc = lax.fori_loop(0, N_FFN_CHUNKS, body, jnp.zeros(lat.shape, F32), unroll=True)
    o_ref[0] = lat + m_ref[0, 5:6, :] * _rms(acc, g_ref[3:4, :])


def _post_ffn(x, y, mod, g4, wmix, win, wout, *, tm, tmajor=False):
    bsz, length, d = x.shape
    width = y.shape[-1] // bsz if tmajor else y.shape[-1]
    full = lambda shape: pl.BlockSpec(shape, lambda b, i: (0,) * len(shape))
    in_specs = [_seq_spec(False, tm, d), _seq_spec(tmajor, tm, width), _mod_spec(mod), full((4, d))]
    args = [x, y, mod, g4]
    if wmix is not None:
        in_specs.append(full(wmix.shape))
        args.append(wmix)
    in_specs += [full(win.shape), full(wout.shape)]
    args += [win, wout]
    return pl.pallas_call(
        functools.partial(_ffn_kernel, has_wmix=wmix is not None),
        grid=(bsz, length // tm),
        in_specs=in_specs,
        out_specs=_seq_spec(False, tm, d),
        out_shape=jax.ShapeDtypeStruct(x.shape, F32),
        compiler_params=_params("parallel", "parallel"),
        name="post_ffn",
    )(*args)


def _ffn_weights(w_in, w_out):
    d = w_in.shape[0]
    win = w_in.astype(BF16).reshape(d, 2, N_FFN_CHUNKS, FFN_CHUNK).transpose(1, 2, 0, 3)
    wout = w_out.astype(BF16).reshape(N_FFN_CHUNKS, FFN_CHUNK, d)
    return win, wout


def _s5_disc_kernel(lre_ref, lim_ref, ldt_ref, bre_ref, bim_ref, lbr_ref, lbi_ref, bbr_ref, bbi_ref):
    lre, lim = lre_ref[...], lim_ref[...]
    dt = jnp.exp(ldt_ref[...])
    mag = jnp.exp(lre * dt)
    lbr = mag * jnp.cos(lim * dt)
    lbi = mag * jnp.sin(lim * dt)
    lbr_ref[...] = lbr
    lbi_ref[...] = lbi
    nr = lbr - 1.0
    den = lre * lre + lim * lim
    cr = (nr * lre + lbi * lim) / den
    ci = (lbi * lre - nr * lim) / den
    for dr in range(2):
        bbr_ref[dr] = cr[dr:dr + 1] * bre_ref[...] - ci[dr:dr + 1] * bim_ref[...]
        bbi_ref[dr] = cr[dr:dr + 1] * bim_ref[...] + ci[dr:dr + 1] * bre_ref[...]


def _s5_discretise(lam_re, lam_im, log_dt, b_re, b_im):
    gp = S5_GROUPS * S5_STATE
    flat = lambda t: t.astype(F32).reshape(2, gp)
    ldt = jnp.repeat(log_dt.astype(F32), S5_STATE, axis=1)
    bt = lambda t: t.astype(F32).reshape(gp, S5_GROUP).T
    shp = jax.ShapeDtypeStruct
    return pl.pallas_call(
        _s5_disc_kernel,
        out_shape=(shp((2, gp), F32), shp((2, gp), F32),
                   shp((2, S5_GROUP, gp), F32), shp((2, S5_GROUP, gp), F32)),
        name="s5_discretise",
    )(flat(lam_re), flat(lam_im), ldt, bt(b_re), bt(b_im))


def _s5_scan_kernel(u_ref, wb_ref, lam_ref, wc_ref, x0_ref, y_ref, xt_ref, bu_scr, st_scr, *, reverse, nb):
    i = pl.program_id(1)

    @pl.when(i == 0)
    def _():
        st_scr[...] = x0_ref[...]

    for kt in range(S5_KT):
        u = u_ref[0, :, kt * LANES:(kt + 1) * LANES].astype(BF16)
        bu_scr[kt] = jnp.dot(u, wb_ref[kt], preferred_element_type=F32)
    for kt in range(S5_KT):
        lr = jnp.broadcast_to(lam_ref[kt, 0], (nb, S5_SL))
        li = jnp.broadcast_to(lam_ref[kt, 1], (nb, S5_SL))
        xr, xi = st_scr[kt, 0], st_scr[kt, 1]
        for s in range(S5_TB):
            r0 = ((S5_TB - 1 - s) if reverse else s) * nb
            xr, xi = (lr * xr - li * xi + bu_scr[kt, r0:r0 + nb, 0:S5_SL],
                      lr * xi + li * xr + bu_scr[kt, r0:r0 + nb, S5_SL:2 * S5_SL])
            bu_scr[kt, r0:r0 + nb, 0:S5_SL] = xr
            bu_scr[kt, r0:r0 + nb, S5_SL:2 * S5_SL] = xi
        st_scr[kt, 0] = xr
        st_scr[kt, 1] = xi
        y_ref[0, :, kt * LANES:(kt + 1) * LANES] = jnp.dot(bu_scr[kt].astype(BF16), wc_ref[kt],
                                                            preferred_element_type=F32)

    @pl.when(i == pl.num_programs(1) - 1)
    def _():
        xt_ref[...] = st_scr[...]


def _s5_scan(u, wb, lam, wc, x0, *, reverse, nb):
    _, rows, d = u.shape
    nt = rows // (S5_TB * nb)
    tix = (lambda i: nt - 1 - i) if reverse else (lambda i: i)
    blk = S5_TB * nb
    shp = jax.ShapeDtypeStruct
    return pl.pallas_call(
        functools.partial(_s5_scan_kernel, reverse=reverse, nb=nb),
        grid=(S5_NT // S5_KT, nt),
        in_specs=[pl.BlockSpec((1, blk, S5_KT * LANES), lambda k, i: (0, tix(i), k)),
                  pl.BlockSpec((S5_KT, LANES, 2 * S5_SL), lambda k, i: (k, 0, 0)),
                  pl.BlockSpec((S5_KT, 2, 1, S5_SL), lambda k, i: (k, 0, 0, 0)),
                  pl.BlockSpec((S5_KT, 2 * S5_SL, LANES), lambda k, i: (k, 0, 0)),
                  pl.BlockSpec((S5_KT, 2, nb, S5_SL), lambda k, i: (k, 0, 0, 0))],
        out_specs=[pl.BlockSpec((1, blk, S5_KT * LANES), lambda k, i: (0, tix(i), k)),
                   pl.BlockSpec((S5_KT, 2, nb, S5_SL), lambda k, i: (k, 0, 0, 0))],
        out_shape=(shp((1, rows, d), F32), shp((S5_NT, 2, nb, S5_SL), F32)),
        scratch_shapes=[pltpu.VMEM((S5_KT, blk, 2 * S5_SL), F32), pltpu.VMEM((S5_KT, 2, nb, S5_SL), F32)],
        compiler_params=_params("parallel", "arbitrary"),
        name="s5_scan",
    )(u, wb, lam, wc, x0)


def _s5_out_kernel(yf_ref, yb_ref, u_ref, d_ref, w_ref, o_ref):
    y = yf_ref[0] + yb_ref[0] + d_ref[...] * u_ref[0].astype(F32)
    a = jax.nn.gelu(y).astype(BF16)
    val = jnp.dot(a, w_ref[:, 0:D_MODEL], preferred_element_type=F32)
    gate = jnp.dot(a, w_ref[:, D_MODEL:2 * D_MODEL], preferred_element_type=F32)
    o_ref[0] = (val * jax.nn.sigmoid(gate)).astype(o_ref.dtype)


def _s5_out(y_f, y_b, u, d_skip, w_glu, *, tm):
    _, rows, d = u.shape
    row_spec = pl.BlockSpec((1, tm, d), lambda i: (0, i, 0))
    return pl.pallas_call(
        _s5_out_kernel,
        grid=(rows // tm,),
        in_specs=[row_spec, row_spec, row_spec,
                  pl.BlockSpec((1, d), lambda i: (0, 0)),
                  pl.BlockSpec((d, 2 * d), lambda i: (0, 0))],
        out_specs=row_spec,
        out_shape=jax.ShapeDtypeStruct((1, rows, d), BF16),
        compiler_params=_params("parallel"),
        name="s5_out",
    )(y_f, y_b, u, d_skip.astype(F32).reshape(1, d), w_glu.astype(BF16))


def _s5_mixer(u_lat, u_ctx, nb, lam_re, lam_im, log_dt, b_re, b_im, c_re, c_im, d_skip, w_glu):
    lbr, lbi, bbr, bbi = _s5_discretise(lam_re, lam_im, log_dt, b_re, b_im)
    eye = jnp.eye(S5_TILE_G, dtype=F32)
    bb = jnp.stack([bbr, bbi], axis=1).reshape(2, 2, S5_GROUP, S5_NT, S5_TILE_G, S5_STATE)
    wb = jnp.einsum('dzcktp,ts->dktczsp', bb, eye).reshape(2, S5_NT, LANES, 2 * S5_SL).astype(BF16)
    cc = jnp.stack([c_re.astype(F32), -c_im.astype(F32)], axis=0).reshape(2, S5_NT, S5_TILE_G, S5_GROUP, S5_STATE)
    wc = jnp.einsum('zktcp,ts->kzsptc', cc, eye).reshape(S5_NT, 2 * S5_SL, LANES).astype(BF16)
    lam = jnp.stack([lbr, lbi], axis=1).reshape(2, 2, S5_NT, 1, S5_SL).transpose(0, 2, 1, 3, 4)
    x0 = jnp.zeros((S5_NT, 2, nb, S5_SL), F32)
    yc_f, x_f = _s5_scan(u_ctx, wb[0], lam[0], wc, x0, reverse=False, nb=nb)
    yc_b, x_b = _s5_scan(u_ctx, wb[1], lam[1], wc, x0, reverse=True, nb=nb)
    yl_f, _ = _s5_scan(u_lat, wb[0], lam[0], wc, x_f, reverse=False, nb=nb)
    yl_b, _ = _s5_scan(u_lat, wb[1], lam[1], wc, x_b, reverse=True, nb=nb)
    y_lat = _s5_out(yl_f, yl_b, u_lat, d_skip, w_glu, tm=512)
    y_ctx = _s5_out(yc_f, yc_b, u_ctx, d_skip, w_glu, tm=512)
    return y_lat, y_ctx


_NT_DIMS = (((1,), (1,)), ((), ()))
_TN_DIMS = (((0,), (0,)), ((), ()))


def _tri_mask(n, reverse):
    r = lax.broadcasted_iota(jnp.int32, (n, n), 0)
    c = lax.broadcasted_iota(jnp.int32, (n, n), 1)
    return (c >= r) if reverse else (c <= r)


def _cumsum_scan(g, reverse):
    n = g.shape[0]
    rows = lax.broadcasted_iota(jnp.int32, (n, 1), 0)
    d = 1
    while d < n:
        if reverse:
            shifted = jnp.where(rows < n - d, pltpu.roll(g, n - d, axis=0), 0.0)
        else:
            shifted = jnp.where(rows >= d, pltpu.roll(g, d, axis=0), 0.0)
        g = g + shifted
        d *= 2
    return g


def _cumsum_f32(mask_bf16, g):
    g1 = g.astype(BF16)
    r1 = g - g1.astype(F32)
    g2 = r1.astype(BF16)
    g3 = (r1 - g2.astype(F32)).astype(BF16)
    dot = lambda t: jnp.dot(mask_bf16, t, preferred_element_type=F32)
    return dot(g1) + dot(g2) + dot(g3)


def _gla_scan_kernel(q_ref, k_ref, v_ref, glr_ref, wg_ref, bg_ref, s0_ref, o_ref, s_ref, *, reverse):
    @pl.when(pl.program_id(1) == 0)
    def _():
        s_ref[...] = s0_ref[...]

    nb, ch = q_ref.shape[0], q_ref.shape[1]
    mask = _tri_mask(ch, reverse)
    end = 0 if reverse else ch - 1
    glr = glr_ref[...].reshape(nb * ch, LANES).astype(BF16)
    pre_all = jnp.dot(glr, wg_ref[...], preferred_element_type=F32) + bg_ref[...]
    pairs = [(r, h) for r in range(nb) for h in range(GLA_HEADS)]
    sk = lambda h: slice(h * GLA_DK, (h + 1) * GLA_DK)
    sv = lambda h: slice(h * GLA_DV, (h + 1) * GLA_DV)
    q_dec, k_dec, k_end, decay_end = [], [], [], []
    for r in range(nb):
        g = jax.nn.log_sigmoid(pre_all[r * ch:(r + 1) * ch]) / GLA_GATE_NORM
        b = _cumsum_scan(g, reverse)
        b_end = b[end:end + 1, :]
        k = k_ref[r].astype(F32)
        q_dec.append((q_ref[r].astype(F32) * GLA_DK ** -0.5 * jnp.exp(b)).astype(BF16))
        k_dec.append((k * jnp.exp(-b)).astype(BF16))
        k_end.append((k * jnp.exp(b_end - b)).astype(BF16))
        decay_end.append(jnp.exp(b_end))
    scores = [lax.dot_general(q_dec[r][:, sk(h)], k_dec[r][:, sk(h)], _NT_DIMS, preferred_element_type=F32)
              for r, h in pairs]
    scores = [jnp.where(mask, s, 0.0).astype(BF16) for s in scores]
    states = [s_ref[r, h] for r, h in pairs]
    inter = [lax.dot_general(q_dec[r][:, sk(h)], st.astype(BF16), _NT_DIMS, preferred_element_type=F32)
             for (r, h), st in zip(pairs, states)]
    intra = [jnp.dot(s, v_ref[r, :, sv(h)], preferred_element_type=F32) for (r, h), s in zip(pairs, scores)]
    upd = [lax.dot_general(v_ref[r, :, sv(h)], k_end[r][:, sk(h)], _TN_DIMS, preferred_element_type=F32)
           for r, h in pairs]
    for i, (r, h) in enumerate(pairs):
        o_ref[r, :, sv(h)] = (intra[i] + inter[i]).astype(o_ref.dtype)
        s_ref[r, h] = decay_end[r][:, sk(h)] * states[i] + upd[i]


def _gla_scan(z, glr, wg, bg, s0, *, reverse, ch):
    bsz, length, _ = z.shape
    nc = length // ch
    nb = GLA_ROWS if bsz % GLA_ROWS == 0 else 1
    cix = (lambda i: nc - 1 - i) if reverse else (lambda i: i)
    st_spec = pl.BlockSpec((nb, GLA_HEADS, GLA_DV, GLA_DK), lambda b, i: (b, 0, 0, 0))
    shp = jax.ShapeDtypeStruct
    return pl.pallas_call(
        functools.partial(_gla_scan_kernel, reverse=reverse),
        grid=(bsz // nb, nc),
        in_specs=[pl.BlockSpec((nb, ch, GLA_QK), lambda b, i: (b, cix(i), 0)),
                  pl.BlockSpec((nb, ch, GLA_QK), lambda b, i: (b, cix(i), 1)),
                  pl.BlockSpec((nb, ch, GLA_VD), lambda b, i: (b, cix(i), 1)),
                  pl.BlockSpec((nb, ch, LANES), lambda b, i: (b, cix(i), 0)),
                  pl.BlockSpec((LANES, GLA_QK), lambda b, i: (0, 0)),
                  pl.BlockSpec((1, GLA_QK), lambda b, i: (0, 0)),
                  st_spec],
        out_specs=[pl.BlockSpec((nb, ch, GLA_VD), lambda b, i: (b, cix(i), 0)), st_spec],
        out_shape=[shp((bsz, length, GLA_VD), BF16), shp(s0.shape, F32)],
        compiler_params=_params("parallel", "arbitrary"),
        name="gla_scan",
    )(z, z, z, glr, wg, bg, s0)


def _gla_out_kernel(of_ref, ob_ref, r_ref, ng_ref, o_ref):
    for h in range(GLA_HEADS):
        sv = slice(h * GLA_DV, (h + 1) * GLA_DV)
        o = of_ref[0, :, sv].astype(F32) + ob_ref[0, :, sv].astype(F32)
        o = _rms(o, ng_ref[...])
        o_ref[0, :, sv] = (o * _silu(r_ref[0, :, sv].astype(F32))).astype(o_ref.dtype)


def _gla_out(o_f, o_b, z, norm_g, *, tm):
    bsz, length, _ = o_f.shape
    seq = pl.BlockSpec((1, tm, GLA_VD), lambda b, i: (b, i, 0))
    return pl.pallas_call(
        _gla_out_kernel,
        grid=(bsz, length // tm),
        in_specs=[seq, seq, pl.BlockSpec((1, tm, D_MODEL), lambda b, i: (b, i, 2)),
                  pl.BlockSpec((1, GLA_DV), lambda b, i: (0, 0))],
        out_specs=seq,
        out_shape=jax.ShapeDtypeStruct((bsz, length, GLA_VD), BF16),
        compiler_params=_params("parallel", "parallel"),
        name="gla_out",
    )(o_f, o_b, z, norm_g.astype(F32).reshape(1, GLA_DV))


def _gla_weights(w_in, w_gate, b_gate):
    n_main = 2 * GLA_QK + GLA_VD + D_MODEL
    w_main = w_in[:, :n_main].astype(BF16)
    w_glr = jnp.pad(w_in[:, n_main:], ((0, 0), (0, LANES - 2 * GLA_RANK))).astype(BF16)
    wg = jnp.zeros((2, LANES, GLA_QK), F32)
    wg = wg.at[0, :GLA_RANK].set(w_gate[0].astype(F32)).at[1, GLA_RANK:2 * GLA_RANK].set(w_gate[1].astype(F32))
    return w_main, w_glr, wg.astype(BF16), b_gate.astype(F32).reshape(2, 1, GLA_QK)


def _gla_mixer(z_lat, glr_lat, z_ctx, glr_ctx, wg, bg, norm_g):
    bsz = z_lat.shape[0]
    s0 = jnp.zeros((bsz, GLA_HEADS, GLA_DV, GLA_DK), F32)

    def bidirectional(z, glr, s_f, s_b, tm):
        o_f, s_f = _gla_scan(z, glr, wg[0], bg[0], s_f, reverse=False, ch=GLA_CHUNK)
        o_b, s_b = _gla_scan(z, glr, wg[1], bg[1], s_b, reverse=True, ch=GLA_CHUNK)
        return _gla_out(o_f, o_b, z, norm_g, tm=tm), s_f, s_b

    t_ctx, s_f, s_b = bidirectional(z_ctx, glr_ctx, s0, s0, z_ctx.shape[1])
    t_lat, _, _ = bidirectional(z_lat, glr_lat, s_f, s_b, 512)
    return t_lat, t_ctx


def _dot3(a, b):
    a1 = a.astype(BF16)
    a2 = (a - a1.astype(F32)).astype(BF16)
    b1 = b.astype(BF16)
    b2 = (b - b1.astype(F32)).astype(BF16)
    dot = lambda s, t: jnp.dot(s, t, preferred_element_type=F32)
    return dot(a1, b1) + dot(a1, b2) + dot(a2, b1)


def _hy_filter_kernel(bands_ref, w10_ref, w1c_ref, w1s_ref, b1_ref, w2_ref, b2_ref, w3_ref, fr_ref, dl_ref,
                      ha_ref, hb_ref, *, length):
    tm = ha_ref.shape[0]
    n = (pl.program_id(0) * tm + lax.broadcasted_iota(jnp.int32, (tm, 1), 0)).astype(F32)

    def mlp(pos):
        t = pos * (1.0 / length)
        ang = (2.0 * math.pi * t) * bands_ref[...]
        pre = (t * w10_ref[...] + _dot3(jnp.cos(ang), w1c_ref[...]) - _dot3(jnp.sin(ang), w1s_ref[...])
               + b1_ref[...])
        z = jnp.sin(fr_ref[...] * pre)
        z = jnp.sin(fr_ref[...] * (_dot3(z, w2_ref[...]) + b2_ref[...]))
        return z, jnp.exp(-t * dl_ref[...])

    z, dec = mlp(n)
    h_f = _dot3(z, w3_ref[:, 0:D_MODEL]) * dec
    h_b0 = _dot3(z, w3_ref[:, D_MODEL:2 * D_MODEL]) * dec
    ha_ref[...] = (h_f + jnp.where(n == 0.0, h_b0, 0.0)).astype(ha_ref.dtype)
    zb, decb = mlp(length - n)
    h_b = _dot3(zb, w3_ref[:, D_MODEL:2 * D_MODEL]) * decb
    hb_ref[...] = jnp.where(n > 0.0, h_b, 0.0).astype(hb_ref.dtype)


def _hy_filter(length, w1, b1, w2, b2, w3, freq):
    f32 = lambda t: t.astype(F32)
    nb = HY_BANDS
    bands = jnp.pad(jnp.linspace(1e-4, nb - 1, nb, dtype=F32), (0, LANES - nb)).reshape(1, LANES)
    w1 = f32(w1)
    w1c = jnp.pad(w1[1:1 + nb], ((0, LANES - nb), (0, 0)))
    w1s = jnp.pad(w1[1 + nb:1 + 2 * nb], ((0, LANES - nb), (0, 0)))
    log_target = math.log(HY_DECAY_TARGET)
    deltas = jnp.abs(jnp.linspace(log_target / HY_SLOW_DECAY, log_target / HY_FAST_DECAY, D_MODEL, dtype=F32))
    tm = min(length, 512)
    args = [bands, w1[0:1], w1c, w1s, f32(b1).reshape(1, -1), f32(w2), f32(b2).reshape(1, -1), f32(w3),
            f32(freq).reshape(1, -1), deltas.reshape(1, D_MODEL)]
    out_spec = pl.BlockSpec((tm, D_MODEL), lambda i: (i, 0))
    return pl.pallas_call(
        functools.partial(_hy_filter_kernel, length=length),
        grid=(length // tm,),
        in_specs=[pl.BlockSpec(a.shape, lambda i: (0, 0)) for a in args],
        out_specs=[out_spec, out_spec],
        out_shape=[jax.ShapeDtypeStruct((length, D_MODEL), BF16)] * 2,
        compiler_params=_params("parallel"),
        name="hy_filter",
    )(*args)


def _conv3(x_ref, xp_ref, xn_ref, cw_ref, cb_ref, r, nr):
    tm = x_ref.shape[1]
    x = x_ref[0].astype(F32)
    prev_row = jnp.where(r > 0, xp_ref[0][HALO - 1:HALO, :].astype(F32), 0.0)
    next_row = jnp.where(r < nr - 1, xn_ref[0][0:1, :].astype(F32), 0.0)
    rows = lax.broadcasted_iota(jnp.int32, (tm, 1), 0)
    x_prev = jnp.where(rows == 0, prev_row, pltpu.roll(x, 1, axis=0))
    x_next = jnp.where(rows == tm - 1, next_row, pltpu.roll(x, tm - 1, axis=0))
    return cw_ref[0:1, :] * x_prev + cw_ref[1:2, :] * x + cw_ref[2:3, :] * x_next + cb_ref[...]


def _conv3_specs(tm, length, ct, lane_block):
    hb, nhb = tm // HALO, length // HALO
    return [pl.BlockSpec((1, tm, ct), lambda b, r, j: (b, r, lane_block(j))),
            pl.BlockSpec((1, HALO, ct), lambda b, r, j: (b, jnp.maximum(r * hb - 1, 0), lane_block(j))),
            pl.BlockSpec((1, HALO, ct), lambda b, r, j: (b, jnp.minimum((r + 1) * hb, nhb - 1), lane_block(j))),
            pl.BlockSpec((3, ct), lambda b, r, j: (0, lane_block(j))),
            pl.BlockSpec((1, ct), lambda b, r, j: (0, lane_block(j)))]


def _hy_conv_kernel(*refs):
    r, nr = pl.program_id(1), pl.num_programs(1)
    u = [_conv3(*refs[5 * a:5 * a + 5], r, nr) for a in range(3)]
    x0_ref, p_ref = refs[15:]
    x0_ref[0] = u[0].astype(x0_ref.dtype)
    p_ref[0] = (u[2] * u[1]).astype(p_ref.dtype)


def _hy_conv(z, conv_w, conv_b, *, tm):
    bsz, length, _ = z.shape
    nj = D_MODEL // HY_CT
    cw, cb = conv_w.astype(F32), conv_b.astype(F32).reshape(1, -1)
    in_specs, args = [], []
    for a in range(3):
        in_specs += _conv3_specs(tm, length, HY_CT, lambda j, a=a: a * nj + j)
        args += [z, z, z, cw, cb]
    out_spec = pl.BlockSpec((1, tm, HY_CT), lambda b, r, j: (b, r, j))
    return pl.pallas_call(
        _hy_conv_kernel,
        grid=(bsz, length // tm, nj),
        in_specs=in_specs,
        out_specs=[out_spec, out_spec],
        out_shape=[jax.ShapeDtypeStruct((bsz, length, D_MODEL), BF16)] * 2,
        compiler_params=_params("parallel", "parallel", "parallel"),
        name="hy_conv",
    )(*args)


def _dft_cs(rows, cols, modulus):
    m = (rows.astype(jnp.int32)[:, None] * cols.astype(jnp.int32)[None, :]) % modulus
    theta = m.astype(F32) * (2.0 * math.pi / modulus)
    return jnp.cos(theta), jnp.sin(theta)


def _cplx_rows(c, s):
    return jnp.concatenate([jnp.concatenate([c, s], axis=1), jnp.concatenate([-s, c], axis=1)], axis=0)


def _hy_s1_kernel(*refs, real):
    if real:
        zr_ref, f_ref, o_ref = refs
        z = zr_ref[0]
    else:
        zr_ref, zi_ref, f_ref, o_ref = refs
        z = jnp.concatenate([zr_ref[0], zi_ref[0]], axis=0)
    res = jnp.dot(f_ref[...], z, preferred_element_type=F32)
    o_ref[0, 0] = res[:HY_N1].astype(o_ref.dtype)
    o_ref[0, 1] = res[HY_N1:].astype(o_ref.dtype)


def _hy_stage1(p, f, *, real):
    nseq, length, d = p.shape
    n1h = HY_N1 // 2
    cols = (length // n1h) * d
    pv = p.reshape(nseq, n1h, cols)
    npair = nseq if real else nseq // 2
    tc = HY_TC
    in_specs = [pl.BlockSpec((1, n1h, tc), lambda q, j: (q, 0, j))]
    args = [pv]
    if not real:
        in_specs.append(pl.BlockSpec((1, n1h, tc), lambda q, j: (q + npair, 0, j)))
        args.append(pv)
    in_specs.append(pl.BlockSpec(f.shape, lambda q, j: (0, 0)))
    return pl.pallas_call(
        functools.partial(_hy_s1_kernel, real=real),
        grid=(npair, cols // tc),
        in_specs=in_specs,
        out_specs=pl.BlockSpec((1, 2, HY_N1, tc), lambda q, j: (q, 0, 0, j)),
        out_shape=jax.ShapeDtypeStruct((npair, 2, HY_N1, cols), BF16),
        compiler_params=_params("parallel", "parallel"),
        name="hy_stage1",
    )(*args, f)


def _cmul(x, h, n):
    xr, xi, hr, hi = x[:n], x[n:], h[:n], h[n:]
    return jnp.concatenate([xr * hr - xi * hi, xr * hi + xi * hr], axis=0)


def _hy_s2_kernel(a_ref, g_ref, gi_ref, h_ref, o_ref):
    fine_dft = lambda ref, q: jnp.dot(g_ref[0], jnp.concatenate([ref[q, 0, 0], ref[q, 1, 0]], axis=0),
                                      preferred_element_type=F32)
    pairs = range(a_ref.shape[0])
    hs = [fine_dft(h_ref, q) for q in range(2)]
    x = [fine_dft(a_ref, q) for q in pairs]
    sign = (1 - 2 * (pl.program_id(0) % 2)).astype(F32)
    h = hs[0] + sign * hs[1]
    y = [_cmul(x[q], h, HY_N2).astype(BF16) for q in pairs]
    b = [jnp.dot(gi_ref[0], y[q], preferred_element_type=F32) for q in pairs]
    for q in pairs:
        o_ref[q, 0, 0] = b[q][:HY_N2].astype(o_ref.dtype)
        o_ref[q, 1, 0] = b[q][HY_N2:].astype(o_ref.dtype)


def _hy_stage2(a, g, gi, ah):
    npair = a.shape[0]
    view = lambda t: t.reshape(t.shape[0], 2, HY_N1, HY_N2, D_MODEL)
    av = view(a)
    blk = pl.BlockSpec((npair, 2, 1, HY_N2, D_MODEL), lambda k: (0, 0, k, 0, 0))
    mat = pl.BlockSpec((1, 2 * HY_N2, 2 * HY_N2), lambda k: (k, 0, 0))
    out = pl.pallas_call(
        _hy_s2_kernel,
        grid=(HY_N1,),
        in_specs=[blk, mat, mat,
                  pl.BlockSpec((2, 2, 1, HY_N2, D_MODEL), lambda k: (0, 0, k, 0, 0))],
        out_specs=blk,
        out_shape=jax.ShapeDtypeStruct(av.shape, BF16),
        compiler_params=_params("parallel"),
        name="hy_stage2",
    )(av, g, gi, view(ah))
    return out.reshape(a.shape)


def _hy_s1inv_kernel(b_ref, f_ref, x0_ref, p_ref, bias_ref, o_ref):
    bc = jnp.concatenate([b_ref[0, 0], b_ref[0, 1]], axis=0)
    y = jnp.dot(f_ref[...], bc, preferred_element_type=F32)
    n1h = HY_N1 // 2
    for half in range(2):
        conv = y[half * n1h:(half + 1) * n1h]
        u = p_ref[half, 0].astype(F32)
        o_ref[half, 0] = (x0_ref[half, 0].astype(F32) * (conv + u * bias_ref[...])).astype(o_ref.dtype)


def _hy_stage1_inv(b, f, x0, p, bias):
    npair, _, _, cols = b.shape
    bsz, length, d = x0.shape
    n1h = HY_N1 // 2
    tc = HY_TC
    view = lambda t: t.reshape(2, npair, n1h, cols)
    seq = pl.BlockSpec((2, 1, n1h, tc), lambda q, j: (0, q, 0, j))
    out = pl.pallas_call(
        _hy_s1inv_kernel,
        grid=(npair, cols // tc),
        in_specs=[pl.BlockSpec((1, 2, HY_N1, tc), lambda q, j: (q, 0, 0, j)),
                  pl.BlockSpec(f.shape, lambda q, j: (0, 0)),
                  seq, seq,
                  pl.BlockSpec((1, tc), lambda q, j: (0, 0))],
        out_specs=seq,
        out_shape=jax.ShapeDtypeStruct((2, npair, n1h, cols), BF16),
        compiler_params=_params("parallel", "parallel"),
        name="hy_stage1_inv",
    )(b, f, view(x0), view(p), jnp.tile(bias.astype(F32).reshape(1, d), (1, tc // d)))
    return out.reshape(bsz, length, d)


def _hy_ctx_filter_kernel(ha_ref, hb_ref, f_ref, o_ref):
    h2 = jnp.concatenate([ha_ref[...], hb_ref[...]], axis=0)
    o_ref[...] = jnp.dot(f_ref[...], h2, preferred_element_type=F32)


def _hy_ctx_kernel(zr_ref, zi_ref, f_ref, fi_ref, h_ref, x0_ref, p_ref, bias_ref, o_ref):
    length = zr_ref.shape[1]
    z = jnp.concatenate([zr_ref[0], zi_ref[0]], axis=0)
    x = jnp.dot(f_ref[...], z, preferred_element_type=F32)
    y = _cmul(x, h_ref[...], 2 * length).astype(BF16)
    out = jnp.dot(fi_ref[...], y, preferred_element_type=F32)
    for half in range(2):
        conv = out[half * length:(half + 1) * length]
        u = p_ref[half, 0].astype(F32)
        o_ref[half, 0] = (x0_ref[half, 0].astype(F32) * (conv + u * bias_ref[...])).astype(o_ref.dtype)


def _hy_ctx_conv(x0, p, ha, hb, bias):
    bsz, length, d = p.shape
    npair = bsz // 2
    n = 2 * length
    k = jnp.arange(n)
    c, s = _dft_cs(k, jnp.arange(length), n)
    f = _cplx_rows(c, s).astype(BF16)
    fi = (_cplx_rows(c.T, -s.T) * (1.0 / n)).astype(BF16)
    cf, sf = _dft_cs(k, k, n)
    ffilt = jnp.concatenate([cf, -sf], axis=0).astype(BF16)
    full = lambda t: pl.BlockSpec(t.shape, lambda *_: (0,) * t.ndim)
    hspec = pl.pallas_call(
        _hy_ctx_filter_kernel,
        out_shape=jax.ShapeDtypeStruct((2 * n, d), F32),
        compiler_params=_params(),
        name="hy_ctx_filter",
    )(ha, hb, ffilt)
    view = lambda t: t.reshape(2, npair, length, d)
    seq = pl.BlockSpec((2, 1, length, d), lambda q: (0, q, 0, 0))
    bias2 = bias.astype(F32).reshape(1, d)
    out = pl.pallas_call(
        _hy_ctx_kernel,
        grid=(npair,),
        in_specs=[pl.BlockSpec((1, length, d), lambda q: (q, 0, 0)),
                  pl.BlockSpec((1, length, d), lambda q: (q + npair, 0, 0)),
                  full(f), full(fi), full(hspec), seq, seq, full(bias2)],
        out_specs=seq,
        out_shape=jax.ShapeDtypeStruct((2, npair, length, d), BF16),
        compiler_params=_params("parallel"),
        name="hy_ctx_conv",
    )(p, p, f, fi, hspec, view(x0), view(p), bias2)
    return out.reshape(bsz, length, d)


def _hy_lat_conv(x0, p, ha, hb, bias):
    bsz, length, d = p.shape
    assert 2 * length == HY_N1 * HY_N2 and d == D_MODEL and bsz % 2 == 0
    n = 2 * length
    k1, n1 = jnp.arange(HY_N1), jnp.arange(HY_N1 // 2)
    c1, s1 = _dft_cs(k1, n1, HY_N1)
    f1 = _cplx_rows(c1, s1).astype(BF16)
    f1_real = jnp.concatenate([c1, -s1], axis=0).astype(BF16)
    f1_inv = (_cplx_rows(c1.T, -s1.T) * (1.0 / n)).astype(BF16)
    k = (k1[:, None] + HY_N1 * jnp.arange(HY_N2)[None, :]).reshape(-1)
    c2, s2 = _dft_cs(k, jnp.arange(HY_N2), n)
    c2, s2 = c2.reshape(HY_N1, HY_N2, HY_N2), s2.reshape(HY_N1, HY_N2, HY_N2)
    g = jax.vmap(_cplx_rows)(c2, s2).astype(BF16)
    gi = jax.vmap(_cplx_rows)(c2.transpose(0, 2, 1), -s2.transpose(0, 2, 1)).astype(BF16)
    hfilt = jnp.stack([ha, hb], axis=0)
    a = _hy_stage1(p, f1, real=False)
    b = _hy_stage2(a, g, gi, _hy_stage1(hfilt, f1_real, real=True))
    return _hy_stage1_inv(b, f1_inv, x0, p, bias)


def _hy_mixer(z_lat, z_ctx, conv_w, conv_b, f_w1, f_b1, f_w2, f_b2, f_w3, f_freq, f_bias):
    def run(z, tm, conv_fn):
        length = z.shape[1]
        x0, p = _hy_conv(z, conv_w, conv_b, tm=tm)
        ha, hb = _hy_filter(length, f_w1, f_b1, f_w2, f_b2, f_w3, f_freq)
        return conv_fn(x0, p, ha, hb, f_bias)

    return run(z_lat, 1024, _hy_lat_conv), run(z_ctx, z_ctx.shape[1], _hy_ctx_conv)


def _ml_qkv_kernel(xm_ref, xp_ref, xn_ref, cw_ref, cb_ref, wq_ref, wk_ref, wv_ref, wg_ref, bg_ref,
                   q_ref, k_ref, v_ref, xc_ref, g_ref):
    j = pl.program_id(2)
    xm = xm_ref[0]
    conv = _conv3(xm_ref, xp_ref, xn_ref, cw_ref, cb_ref, pl.program_id(1), pl.num_programs(1))
    xc = _silu(conv).astype(BF16)
    subs = [slice(s * ML_BD, (s + 1) * ML_BD) for s in range(ML_CT // ML_BD)]
    bd = lambda x, w_ref: jnp.concatenate(
        [jnp.dot(x[:, sl], w_ref[s], preferred_element_type=F32) for s, sl in enumerate(subs)], axis=1)
    q, k, v = bd(xc, wq_ref), bd(xc, wk_ref), bd(xm, wv_ref)
    q, k, v = q.astype(BF16), k.astype(BF16), v.astype(BF16)
    q_ref[0], k_ref[0], v_ref[0], xc_ref[0] = q, k, v, xc
    contrib = (jnp.dot(q, wg_ref[0], preferred_element_type=F32)
               + jnp.dot(k, wg_ref[1], preferred_element_type=F32)
               + jnp.dot(v, wg_ref[2], preferred_element_type=F32))

    @pl.when(j == 0)
    def _():
        g_ref[0] = contrib

    @pl.when(j > 0)
    def _():
        g_ref[0] += contrib

    @pl.when(j == pl.num_programs(2) - 1)
    def _():
        g = g_ref[0] + bg_ref[...]
        lane = lax.broadcasted_iota(jnp.int32, g.shape, 1)
        g_ref[0] = jnp.where(lane % (2 * ML_HEADS) >= ML_HEADS, jax.nn.log_sigmoid(g), g)


def _ml_qkv(z, conv_w, conv_b, wq, wk, wv, wg, bg, *, tm):
    bsz, length, _ = z.shape
    nj = ML_INNER // ML_CT
    seq = lambda: pl.BlockSpec((1, tm, ML_CT), lambda b, r, j: (b, r, j))
    shp = jax.ShapeDtypeStruct
    return pl.pallas_call(
        _ml_qkv_kernel,
        grid=(bsz, length // tm, nj),
        in_specs=_conv3_specs(tm, length, ML_CT, lambda j: j) + [
                  pl.BlockSpec((ML_CT // ML_BD, ML_BD, ML_BD), lambda b, r, j: (j, 0, 0)),
                  pl.BlockSpec((ML_CT // ML_BD, ML_BD, ML_BD), lambda b, r, j: (j, 0, 0)),
                  pl.BlockSpec((ML_CT // ML_BD, ML_BD, ML_BD), lambda b, r, j: (j, 0, 0)),
                  pl.BlockSpec((3, ML_CT, LANES), lambda b, r, j: (0, j, 0)),
                  pl.BlockSpec((1, LANES), lambda b, r, j: (0, 0))],
        out_specs=[seq(), seq(), seq(), seq(),
                   pl.BlockSpec((1, tm, LANES), lambda b, r, j: (b, r, 0))],
        out_shape=[shp((bsz, length, ML_INNER), BF16)] * 4 + [shp((bsz, length, LANES), F32)],
        compiler_params=_params("parallel", "parallel", "arbitrary"),
        name="ml_qkv",
    )(z, z, z, conv_w, conv_b, wq, wk, wv, wg, bg)


def _ml_scan_kernel(q_ref, k_ref, v_ref, g_ref, c0_ref, n0_ref, m0_ref, h_ref, c_ref, n_ref, m_ref,
                    *, reverse, z):
    @pl.when(pl.program_id(1) == 0)
    def _():
        c_ref[...] = c0_ref[...]
        n_ref[...] = n0_ref[...]
        m_ref[...] = m0_ref[...]

    ch = q_ref.shape[1]
    mask = _tri_mask(ch, reverse)
    gates = g_ref[0]
    bcum = _cumsum_f32(mask.astype(BF16), gates)
    gates_t, bcum_t = gates.T, bcum.T
    end = 0 if reverse else ch - 1
    for h in range(ML_HEADS):
        li, lf = z * 2 * ML_HEADS + h, z * 2 * ML_HEADS + ML_HEADS + h
        sl = slice(h * ML_DH, (h + 1) * ML_DH)
        q, v = q_ref[0, :, sl], v_ref[0, :, sl]
        k = k_ref[0, :, sl].astype(F32) * ML_DH ** -0.5
        b_col, b_row = bcum[:, lf:lf + 1], bcum_t[lf:lf + 1, :]
        i_col, i_row = gates[:, li:li + 1], gates_t[li:li + 1, :]
        b_end = b_col[end:end + 1, :]
        m_prev = m_ref[0, h][0:1, 0:1]
        d_intra = jnp.where(mask, b_col - b_row + i_row, -jnp.inf)
        d_inter = b_col + m_prev
        m_tok = jnp.maximum(jnp.max(d_intra, axis=1, keepdims=True), d_inter)
        w_inter = jnp.exp(d_inter - m_tok)
        s = lax.dot_general(q, k.astype(BF16), _NT_DIMS, preferred_element_type=F32) * jnp.exp(d_intra - m_tok)
        c_mat = c_ref[0, h]
        n_vec = n_ref[0, h]
        num = (jnp.dot(s.astype(BF16), v, preferred_element_type=F32)
               + w_inter * jnp.dot(q, c_mat.astype(BF16), preferred_element_type=F32))
        den = (jnp.sum(s, axis=1, keepdims=True)
               + w_inter * jnp.sum(q.astype(F32) * n_vec, axis=1, keepdims=True))
        h_ref[0, :, sl] = (num / jnp.maximum(jnp.abs(den), jnp.exp(-m_tok))).astype(h_ref.dtype)
        d_state = b_end - b_col + i_col
        m_new = jnp.maximum(b_end + m_prev, jnp.max(d_state, axis=0, keepdims=True))
        w_prev = jnp.exp(b_end + m_prev - m_new)
        kw = k * jnp.exp(d_state - m_new)
        c_ref[0, h] = w_prev * c_mat + lax.dot_general(kw.astype(BF16), v, _TN_DIMS, preferred_element_type=F32)
        n_ref[0, h] = w_prev * n_vec + jnp.sum(kw, axis=0, keepdims=True)
        m_ref[0, h] = jnp.broadcast_to(m_new, m_ref.shape[2:])


def _ml_scan(q, k, v, gates, state, *, reverse, z, ch):
    bsz, length, _ = q.shape
    nc = length // ch
    cix = (lambda i: nc - 1 - i) if reverse else (lambda i: i)
    seq = lambda w: pl.BlockSpec((1, ch, w), lambda b, i: (b, cix(i), 0))
    st_specs = [pl.BlockSpec((1,) + s.shape[1:], lambda b, i: (b, 0, 0, 0)) for s in state]
    shp = jax.ShapeDtypeStruct
    out = pl.pallas_call(
        functools.partial(_ml_scan_kernel, reverse=reverse, z=z),
        grid=(bsz, nc),
        in_specs=[seq(ML_INNER), seq(ML_INNER), seq(ML_INNER), seq(LANES)] + st_specs,
        out_specs=[seq(ML_INNER)] + st_specs,
        out_shape=[shp((bsz, length, ML_INNER), BF16)] + [shp(s.shape, F32) for s in state],
        compiler_params=_params("parallel", "arbitrary"),
        name="ml_scan",
    )(q, k, v, gates, *state)
    return out[0], tuple(out[1:])


def _ml_out_kernel(hf_ref, hb_ref, xc_ref, z_ref, ng_ref, sk_ref, o_ref):
    for h in range(ML_HEADS):
        sl = slice(h * ML_DH, (h + 1) * ML_DH)
        o = hf_ref[0, :, sl].astype(F32) + hb_ref[0, :, sl].astype(F32)
        o = o - jnp.mean(o, axis=-1, keepdims=True)
        o = o * lax.rsqrt(jnp.mean(o * o, axis=-1, keepdims=True) + EPS) * ng_ref[:, sl]
        t = (o + sk_ref[:, sl] * xc_ref[0, :, sl].astype(F32)) * _silu(z_ref[0, :, sl].astype(F32))
        o_ref[0, :, sl] = t.astype(o_ref.dtype)


def _ml_out(h_f, h_b, xc, z, norm_g, skip, *, tm):
    bsz, length, _ = h_f.shape
    seq = pl.BlockSpec((1, tm, ML_INNER), lambda b, i: (b, i, 0))
    vec = pl.BlockSpec((1, ML_INNER), lambda b, i: (0, 0))
    return pl.pallas_call(
        _ml_out_kernel,
        grid=(bsz, length // tm),
        in_specs=[seq, seq, seq, pl.BlockSpec((1, tm, ML_INNER), lambda b, i: (b, i, 1)), vec, vec],
        out_specs=seq,
        out_shape=jax.ShapeDtypeStruct((bsz, length, ML_INNER), BF16),
        compiler_params=_params("parallel", "parallel"),
        name="ml_out",
    )(h_f, h_b, xc, z, norm_g.astype(F32).reshape(1, ML_INNER), skip.astype(F32).reshape(1, ML_INNER))


def _ml_weights(w_q, w_k, w_v, w_gates, b_gates):
    nb = ML_BD // ML_BLOCK
    eye = jnp.eye(nb, dtype=F32)

    def dense(w):
        w = w.astype(F32).reshape(ML_INNER // ML_BD, nb, ML_BLOCK, ML_BLOCK)
        return jnp.einsum('tncd,nm->tncmd', w, eye).reshape(ML_INNER // ML_BD, ML_BD, ML_BD).astype(BF16)

    ng = 2 * 2 * ML_HEADS
    wg = w_gates.astype(F32).reshape(2, 3, ML_INNER, 2 * ML_HEADS).transpose(1, 2, 0, 3).reshape(3, ML_INNER, ng)
    wg = jnp.pad(wg, ((0, 0), (0, 0), (0, LANES - ng))).astype(BF16)
    bg = jnp.pad(b_gates.astype(F32).reshape(1, ng), ((0, 0), (0, LANES - ng)))
    return dense(w_q), dense(w_k), dense(w_v), wg, bg


def _ml_mixer(z_lat, z_ctx, conv_w, conv_b, w_q, w_k, w_v, w_gates, b_gates, norm_g, skip):
    wq, wk, wv, wg, bg = _ml_weights(w_q, w_k, w_v, w_gates, b_gates)
    cw, cb = conv_w.astype(F32), conv_b.astype(F32).reshape(1, ML_INNER)
    bsz = z_lat.shape[0]
    state = (jnp.zeros((bsz, ML_HEADS, ML_DH, ML_DH), F32),
             jnp.zeros((bsz, ML_HEADS, 1, ML_DH), F32),
             jnp.full((bsz, ML_HEADS, 8, LANES), -jnp.inf, F32))

    def bidirectional(z, st_f, st_b, tm):
        q, k, v, xc, gates = _ml_qkv(z, cw, cb, wq, wk, wv, wg, bg, tm=min(2 * tm, z.shape[1]))
        h_f, st_f = _ml_scan(q, k, v, gates, st_f, reverse=False, z=0, ch=ML_CHUNK)
        h_b, st_b = _ml_scan(q, k, v, gates, st_b, reverse=True, z=1, ch=ML_CHUNK)
        return _ml_out(h_f, h_b, xc, z, norm_g, skip, tm=tm), st_f, st_b

    t_ctx, st_f, st_b = bidirectional(z_ctx, state, state, z_ctx.shape[1])
    t_lat, _, _ = bidirectional(z_lat, st_f, st_b, 512)
    return t_lat, t_ctx


def kernel(x, c, ctx, c_ctx, mod_w, mod_b, norm_g, ffn_w_in, ffn_w_out, gla_w_in, gla_w_gate, gla_b_gate, gla_norm_g, gla_w_out, hy_w_in, hy_conv_w, hy_conv_b, hy_f_w1, hy_f_b1, hy_f_w2, hy_f_b2, hy_f_w3, hy_f_freq, hy_f_bias, hy_w_out, ml_w_in, ml_conv_w, ml_conv_b, ml_w_q, ml_w_k, ml_w_v, ml_w_gates, ml_b_gates, ml_norm_g, ml_skip, ml_w_out, s5_lam_re, s5_lam_im, s5_log_dt, s5_b_re, s5_b_im, s5_c_re, s5_c_im, s5_d, s5_w_glu):
    bsz, seq, d = x.shape
    ctx_len = ctx.shape[1]
    depth = mod_w.shape[0]
    mods = _modulation(c, c_ctx, mod_w, mod_b)
    lat = x.astype(F32)
    cx = ctx.astype(F32)
    tm_lat, tm_ctx = 512, ctx_len
    lat_col_major = False
    for i in range(depth):
        kind, j = i % N_MIXERS, i // N_MIXERS
        last = i == depth - 1
        if (kind >= 2) != lat_col_major:
            lat = _from_col_major(lat) if lat_col_major else _to_col_major(lat)
            lat_col_major = not lat_col_major
        mod_l = mods[i, :bsz]
        mod_c = mods[i, bsz:bsz + 1]
        win, wout = _ffn_weights(ffn_w_in[i], ffn_w_out[i])
        wmix = None
        tmajor = False
        if kind == 3:
            tmajor = True
            u_lat = _norm_mod(lat, mod_l, norm_g[i, 0], tm=tm_lat, tmajor=True, out_dtype=BF16)
            u_ctx = _norm_mod(cx, mod_c, norm_g[i, 0], tm=tm_ctx, tmajor=True, out_dtype=BF16)
            y_lat, y_ctx = _s5_mixer(u_lat.reshape(1, seq * bsz, d), u_ctx.reshape(1, ctx_len * bsz, d), bsz,
                                     s5_lam_re[j], s5_lam_im[j], s5_log_dt[j], s5_b_re[j], s5_b_im[j],
                                     s5_c_re[j], s5_c_im[j], s5_d[j], s5_w_glu[j])
            y_lat = y_lat.reshape(1, seq, bsz * d)
            y_ctx = y_ctx.reshape(1, ctx_len, bsz * d)
        elif kind == 0:
            w_main, w_glr, wg, bg = _gla_weights(gla_w_in[j], gla_w_gate[j], gla_b_gate[j])
            z_lat, glr_lat = _norm_proj(lat, mod_l, norm_g[i, 0], [w_main, w_glr], [BF16, F32], tm=tm_lat)
            z_ctx, glr_ctx = _norm_proj(cx, mod_c, norm_g[i, 0], [w_main, w_glr], [BF16, F32], tm=tm_ctx)
            y_lat, y_ctx = _gla_mixer(z_lat, glr_lat, z_ctx, glr_ctx, wg, bg, gla_norm_g[j])
            wmix = gla_w_out[j].astype(BF16)
        elif kind == 2:
            w_in = ml_w_in[j].astype(BF16)
            (z_lat,) = _norm_proj(lat, mod_l, norm_g[i, 0], [w_in], [BF16], tm=tm_lat)
            (z_ctx,) = _norm_proj(cx, mod_c, norm_g[i, 0], [w_in], [BF16], tm=tm_ctx)
            y_lat, y_ctx = _ml_mixer(z_lat, z_ctx, ml_conv_w[j], ml_conv_b[j], ml_w_q[j], ml_w_k[j], ml_w_v[j],
                                     ml_w_gates[j], ml_b_gates[j], ml_norm_g[j], ml_skip[j])
            wmix = ml_w_out[j].astype(BF16)
        else:
            w_in = hy_w_in[j].astype(BF16)
            (z_lat,) = _norm_proj(lat, mod_l, norm_g[i, 0], [w_in], [BF16], tm=tm_lat)
            (z_ctx,) = _norm_proj(cx, mod_c, norm_g[i, 0], [w_in], [BF16], tm=tm_ctx)
            y_lat, y_ctx = _hy_mixer(z_lat, z_ctx, hy_conv_w[j], hy_conv_b[j], hy_f_w1[j], hy_f_b1[j],
                                     hy_f_w2[j], hy_f_b2[j], hy_f_w3[j], hy_f_freq[j], hy_f_bias[j])
            wmix = hy_w_out[j].astype(BF16)
        lat = _post_ffn(lat, y_lat, mod_l, norm_g[i], wmix, win, wout, tm=tm_lat, tmajor=tmajor)
        if not last:
            cx = _post_ffn(cx, y_ctx, mod_c, norm_g[i], wmix, win, wout, tm=tm_ctx, tmajor=tmajor)
    return _from_col_major(lat) if lat_col_major else lat
```
